```python
import math
import jax, jax.numpy as jnp
from jax import lax
import numpy as np

D_MODEL = 2048
BATCH = 2
SEQ = 8192
DEPTH = 4

D_BRANCH = D_MODEL // 2
DA_HEADS = 8
DA_DV = D_BRANCH // DA_HEADS
DA_DQK = DA_DV // 2
ML_HEADS = 4
ML_DV = D_BRANCH // ML_HEADS
ML_DQK = ML_DV // 2
CONV_W = 4
D_FF = ((8 * D_MODEL // 3 + 255) // 256) * 256
N_EXPERTS = 8
TOP_K = 2
REL_BUCKETS = 32
REL_MAX_DIST = 128
Q_BLOCK = 128
ML_CHUNK = 64
EPS = 1e-6
N_DENSE = (DEPTH + 1) // 2
N_MOE = DEPTH // 2
SPLIT_SIZES = (
    2 * DA_HEADS * DA_DQK,
    2 * DA_HEADS * DA_DQK,
    DA_HEADS * DA_DV,
    ML_HEADS * ML_DQK,
    ML_HEADS * ML_DQK,
    ML_HEADS * ML_DV,
    ML_HEADS * ML_DV,
    ML_HEADS,
    ML_HEADS,
    D_MODEL,
    D_MODEL,
)
D_IN = sum(SPLIT_SIZES)

kernel_name = "hybrid_diffattn_mlstm_moe_block"


def rmsnorm(x, g):
    xf = x.astype(jnp.float32)
    return xf * lax.rsqrt(jnp.mean(xf * xf, axis=-1, keepdims=True) + EPS) * g.astype(jnp.float32)


def swiglu(h, w1, w3, w2):
    return (jax.nn.silu(h @ w1) * (h @ w3)) @ w2


def t5_causal_bucket(rel):
    n = jnp.maximum(rel, 0)
    max_exact = REL_BUCKETS // 2
    large = max_exact + (jnp.log(jnp.maximum(n, 1).astype(jnp.float32) / max_exact)
                         / math.log(REL_MAX_DIST / max_exact)
                         * (REL_BUCKETS - max_exact)).astype(jnp.int32)
    large = jnp.minimum(large, REL_BUCKETS - 1)
    return jnp.where(n < max_exact, n, large)


def diff_attention(q, k, v, rel_table, lam, lam_init, q_g, k_g, sub_g):
    B, T, _ = q.shape
    H2 = 2 * DA_HEADS
    q = rmsnorm(q.reshape(B, T, H2, DA_DQK), q_g) * (DA_DQK ** -0.5)
    k = rmsnorm(k.reshape(B, T, H2, DA_DQK), k_g)
    q = q.transpose(0, 2, 1, 3)
    k = k.transpose(0, 2, 1, 3)
    v = v.astype(jnp.float32).reshape(B, T, DA_HEADS, DA_DV).transpose(0, 2, 1, 3)
    nb = T // Q_BLOCK
    q_blocks = q.reshape(B, H2, nb, Q_BLOCK, DA_DQK).transpose(2, 0, 1, 3, 4)
    kpos = jnp.arange(T, dtype=jnp.int32)
    table = rel_table.astype(jnp.float32)

    def block(args):
        qb, bi = args
        qpos = bi * Q_BLOCK + jnp.arange(Q_BLOCK, dtype=jnp.int32)
        rel = qpos[:, None] - kpos[None, :]
        bias = jnp.transpose(table[t5_causal_bucket(rel)], (2, 0, 1))
        s = jnp.einsum('bhqd,bhkd->bhqk', qb, k) + bias
        s = jnp.where(rel >= 0, s, -jnp.inf)
        p = jax.nn.softmax(s, axis=-1).reshape(B, DA_HEADS, 2, Q_BLOCK, T)
        a = p[:, :, 0] - lam * p[:, :, 1]
        return jnp.einsum('bhqk,bhkv->bhqv', a, v)

    o = lax.map(block, (q_blocks, jnp.arange(nb, dtype=jnp.int32)))
    o = o.transpose(1, 0, 3, 2, 4).reshape(B, T, DA_HEADS, DA_DV)
    o = rmsnorm(o, sub_g) * (1.0 - lam_init)
    return o.reshape(B, T, DA_HEADS * DA_DV)


def causal_depthwise_conv(x, w, b):
    C = x.shape[-1]
    y = lax.conv_general_dilated(
        x.astype(jnp.float32), w.astype(jnp.float32)[:, None, :],
        window_strides=(1,), padding=((CONV_W - 1, 0),),
        dimension_numbers=('NWC', 'WIO', 'NWC'), feature_group_count=C)
    return y + b.astype(jnp.float32)


def mlstm(q, k, v, ig, lf):
    B, T, H, _ = q.shape
    nc = T // ML_CHUNK

    def to_chunks(a):
        return a.reshape(B, nc, ML_CHUNK, H, a.shape[-1]).transpose(1, 0, 3, 2, 4)

    def gate_chunks(a):
        return a.reshape(B, nc, ML_CHUNK, H).transpose(1, 0, 3, 2)

    qc = to_chunks(q.astype(jnp.float32))
    kc = to_chunks(k.astype(jnp.float32) * (ML_DQK ** -0.5))
    vc = to_chunks(v.astype(jnp.float32))
    igc = gate_chunks(ig.astype(jnp.float32))
    lfc = gate_chunks(lf.astype(jnp.float32))
    causal = jnp.tril(jnp.ones((ML_CHUNK, ML_CHUNK), dtype=bool))

    def step(carry, inp):
        C, n, m = carry
        qb, kb, vb, ib, fb = inp
        b = jnp.cumsum(fb, axis=-1)
        dlog = jnp.where(causal, b[..., :, None] - b[..., None, :] + ib[..., None, :], -jnp.inf)
        m_inter = b + m[..., None]
        m_s = jnp.maximum(m_inter, jnp.max(dlog, axis=-1))
        w = jnp.exp(dlog - m_s[..., None]) * jnp.einsum('bhsd,bhjd->bhsj', qb, kb)
        inter = jnp.exp(m_inter - m_s)
        num = (jnp.einsum('bhsj,bhjv->bhsv', w, vb)
               + inter[..., None] * jnp.einsum('bhvd,bhsd->bhsv', C, qb))
        den = jnp.sum(w, axis=-1) + inter * jnp.einsum('bhd,bhsd->bhs', n, qb)
        h = num / jnp.maximum(jnp.abs(den), jnp.exp(-m_s))[..., None]
        m_new = m_s[..., -1]
        w_end = jnp.exp(b[..., -1:] - b + ib - m_new[..., None])
        decay = jnp.exp(b[..., -1] + m - m_new)
        C = decay[..., None, None] * C + jnp.einsum('bhj,bhjv,bhjd->bhvd', w_end, vb, kb)
        n = decay[..., None] * n + jnp.einsum('bhj,bhjd->bhd', w_end, kb)
        return (C, n, m_new), h

    init = (jnp.zeros((B, H, ML_DV, ML_DQK), jnp.float32),
            jnp.zeros((B, H, ML_DQK), jnp.float32),
            jnp.zeros((B, H), jnp.float32))
    _, h = lax.scan(step, init, (qc, kc, vc, igc, lfc))
    return h.transpose(1, 0, 3, 2, 4).reshape(B, T, H, ML_DV)


def moe_swiglu(h, router_w, w1, w3, w2):
    logits = (h @ router_w).astype(jnp.float32)
    top_val, top_idx = lax.top_k(logits, TOP_K)
    gates = jax.nn.softmax(top_val, axis=-1)
    comb = jnp.sum(jax.nn.one_hot(top_idx, N_EXPERTS, dtype=jnp.float32) * gates[..., None], axis=-2)
    out = jnp.zeros(h.shape, jnp.float32)
    for e in range(N_EXPERTS):
        out = out + comb[..., e:e + 1] * swiglu(h, w1[e], w3[e], w2[e])
    return out


def setup_inputs(seed: int = 0) -> dict:
    key = jax.random.key(seed)
    ks = jax.random.split(key, 32)
    f32 = jnp.float32
    nrm = lambda k, shape, s: jax.random.normal(k, shape, f32) * s
    gain = lambda k, shape: 1.0 + 0.02 * jax.random.normal(k, shape, f32)
    D = D_MODEL
    return {
        "x": nrm(ks[0], (BATCH, SEQ, D), 1.0),
        "c": nrm(ks[1], (BATCH, D), 1.0),
        "rel_bias": nrm(ks[2], (REL_BUCKETS, 2 * DA_HEADS), 0.5),
        "w_mod": nrm(ks[3], (DEPTH, D, 6 * D), 0.5 * D ** -0.5),
        "b_mod": nrm(ks[4], (DEPTH, 6 * D), 0.02),
        "norm_mix_g": gain(ks[5], (DEPTH, D)),
        "norm_ff_g": gain(ks[6], (DEPTH, D)),
        "w_in": nrm(ks[7], (DEPTH, D, D_IN), D ** -0.5),
        "da_q_g": gain(ks[8], (DEPTH, DA_DQK)),
        "da_k_g": gain(ks[9], (DEPTH, DA_DQK)),
        "da_lam_q1": nrm(ks[10], (DEPTH, DA_DQK), 0.1),
        "da_lam_k1": nrm(ks[11], (DEPTH, DA_DQK), 0.1),
        "da_lam_q2": nrm(ks[12], (DEPTH, DA_DQK), 0.1),
        "da_lam_k2": nrm(ks[13], (DEPTH, DA_DQK), 0.1),
        "da_sub_g": gain(ks[14], (DEPTH, DA_DV)),
        "ml_conv_w": nrm(ks[15], (DEPTH, CONV_W, 2 * ML_HEADS * ML_DQK), CONV_W ** -0.5),
        "ml_conv_b": nrm(ks[16], (DEPTH, 2 * ML_HEADS * ML_DQK), 0.02),
        "ml_i_bias": nrm(ks[17], (DEPTH, ML_HEADS), 0.1),
        "ml_f_bias": 3.0 + nrm(ks[18], (DEPTH, ML_HEADS), 0.5),
        "ml_norm_g": gain(ks[19], (DEPTH, ML_HEADS * ML_DV)),
        "w_branch_a": nrm(ks[20], (DEPTH, D_BRANCH, D), D_BRANCH ** -0.5),
        "w_branch_b": nrm(ks[21], (DEPTH, D_BRANCH, D), D_BRANCH ** -0.5),
        "w_out": nrm(ks[22], (DEPTH, D, D), D ** -0.5),
        "ffn_w1": nrm(ks[23], (N_DENSE, D, D_FF), D ** -0.5),
        "ffn_w3": nrm(ks[24], (N_DENSE, D, D_FF), D ** -0.5),
        "ffn_w2": nrm(ks[25], (N_DENSE, D_FF, D), D_FF ** -0.5),
        "router_w": nrm(ks[26], (N_MOE, D, N_EXPERTS), D ** -0.5),
        "moe_w1": nrm(ks[27], (N_MOE, N_EXPERTS, D, D_FF), D ** -0.5),
        "moe_w3": nrm(ks[28], (N_MOE, N_EXPERTS, D, D_FF), D ** -0.5),
        "moe_w2": nrm(ks[29], (N_MOE, N_EXPERTS, D_FF, D), D_FF ** -0.5),
    }


def reference(x, c, rel_bias, w_mod, b_mod, norm_mix_g, norm_ff_g, w_in,
              da_q_g, da_k_g, da_lam_q1, da_lam_k1, da_lam_q2, da_lam_k2, da_sub_g,
              ml_conv_w, ml_conv_b, ml_i_bias, ml_f_bias, ml_norm_g,
              w_branch_a, w_branch_b, w_out,
              ffn_w1, ffn_w3, ffn_w2, router_w, moe_w1, moe_w3, moe_w2):
    B, T, _ = x.shape
    split_points = [int(s) for s in np.cumsum(SPLIT_SIZES)[:-1]]
    c_act = jax.nn.silu(c.astype(jnp.float32))
    x = x.astype(jnp.float32)
    for l in range(DEPTH):
        mod = c_act @ w_mod[l] + b_mod[l]
        shift1, scale1, gate1, shift2, scale2, gate2 = jnp.split(mod[:, None, :], 6, axis=-1)

        h = rmsnorm(x, norm_mix_g[l]) * (1.0 + scale1) + shift1
        proj = h @ w_in[l]
        da_q, da_k, da_v, ml_q, ml_k, ml_v, ml_o, ml_i, ml_f, g_a, g_b = jnp.split(proj, split_points, axis=-1)

        lam_init = 0.8 - 0.6 * math.exp(-0.3 * l)
        lam = (jnp.exp(jnp.sum(da_lam_q1[l].astype(jnp.float32) * da_lam_k1[l]))
               - jnp.exp(jnp.sum(da_lam_q2[l].astype(jnp.float32) * da_lam_k2[l])) + lam_init)
        y_a = diff_attention(da_q, da_k, da_v, rel_bias, lam, lam_init, da_q_g[l], da_k_g[l], da_sub_g[l])

        qk = jax.nn.silu(causal_depthwise_conv(jnp.concatenate([ml_q, ml_k], axis=-1), ml_conv_w[l], ml_conv_b[l]))
        mq, mk = jnp.split(qk, 2, axis=-1)
        ig = ml_i + ml_i_bias[l]
        lf = jax.nn.log_sigmoid(ml_f + ml_f_bias[l])
        hm = mlstm(mq.reshape(B, T, ML_HEADS, ML_DQK), mk.reshape(B, T, ML_HEADS, ML_DQK),
                   ml_v.reshape(B, T, ML_HEADS, ML_DV), ig, lf)
        y_b = rmsnorm(hm, ml_norm_g[l].reshape(ML_HEADS, ML_DV)).reshape(B, T, D_BRANCH) * jax.nn.sigmoid(ml_o)

        merged = jax.nn.sigmoid(g_a) * (y_a @ w_branch_a[l]) + jax.nn.sigmoid(g_b) * (y_b @ w_branch_b[l])
        x = x + gate1 * (merged @ w_out[l])

        h = rmsnorm(x, norm_ff_g[l]) * (1.0 + scale2) + shift2
        if l % 2 == 0:
            ff = swiglu(h, ffn_w1[l // 2], ffn_w3[l // 2], ffn_w2[l // 2])
        else:
            ff = moe_swiglu(h, router_w[l // 2], moe_w1[l // 2], moe_w3[l // 2], moe_w2[l // 2])
        x = x + gate2 * ff
    return x
```

```python
import functools
import math
from typing import NamedTuple

import numpy as np
import jax
import jax.numpy as jnp
from jax import lax
from jax.experimental import pallas as pl
from jax.experimental.pallas import tpu as pltpu

F32 = jnp.float32
BF16 = jnp.bfloat16

DA_DQK = 64
DA_DV = 128
ML_DQK = 128
ML_DV = 256
CONV_W = 4
TOP_K = 2
REL_BUCKETS = 32
REL_MAX_DIST = 128
EPS = 1e-6
NEG = -1e30
LANES = 128
SUBLANES = 8
VMEM_LIMIT = 52 * 1024 * 1024


class Dims(NamedTuple):
    d_model: int
    batch: int
    seq: int
    depth: int
    d_ff: int
    n_experts: int
    tm: int = 512
    tn: int = 512
    tq: int = 512
    ml_chunk: int = 256
    tg: int = 512
    tc: int = 256

    @property
    def d_branch(self):
        return self.d_model // 2

    @property
    def da_heads(self):
        return self.d_branch // DA_DV

    @property
    def ml_heads(self):
        return self.d_branch // ML_DV

    @property
    def n_main(self):
        return 3 * self.d_branch + 2 * self.ml_heads * ML_DQK + 2 * self.d_branch

    @property
    def tokens(self):
        return self.batch * self.seq


def _cp(*sem):
    return pltpu.CompilerParams(dimension_semantics=sem, vmem_limit_bytes=VMEM_LIMIT)


def _sigmoid(x):
    return 1.0 / (1.0 + jnp.exp(-x))


def _log_sigmoid(x):
    return jnp.minimum(x, 0.0) - jnp.log(1.0 + jnp.exp(-jnp.abs(x)))


def _mod_kernel(c_ref, w_ref, b_ref, o_ref):
    c = c_ref[...]
    ca = c * _sigmoid(c)
    o_ref[...] = jnp.dot(ca, w_ref[...], preferred_element_type=F32,
                         precision=lax.Precision.HIGHEST) + b_ref[...]


def adaln_mod(c, w_mod, b_mod):
    depth, d, n = w_mod.shape
    b = c.shape[0]
    bp = -(-b // SUBLANES) * SUBLANES
    cp = jnp.zeros((bp, d), F32).at[:b].set(c.astype(F32))
    tn = min(n, 1024)
    out = pl.pallas_call(
        _mod_kernel,
        grid=(depth, n // tn),
        in_specs=[pl.BlockSpec((bp, d), lambda l, j: (0, 0)),
                  pl.BlockSpec((None, d, tn), lambda l, j: (l, 0, j)),
                  pl.BlockSpec((None, 1, tn), lambda l, j: (l, 0, j))],
        out_specs=pl.BlockSpec((None, bp, tn), lambda l, j: (l, 0, j)),
        out_shape=jax.ShapeDtypeStruct((depth, bp, n), F32),
        compiler_params=_cp("parallel", "parallel"),
        name="adaln_mod",
    )(cp, w_mod, b_mod.reshape(depth, 1, n))
    return out[:, :b]


def _norm_mod_kernel(x_ref, g_ref, sc_ref, sh_ref, o_ref):
    x = x_ref[...]
    ms = jnp.mean(x * x, axis=-1, keepdims=True)
    h = x * lax.rsqrt(ms + EPS) * g_ref[...] * (1.0 + sc_ref[...]) + sh_ref[...]
    o_ref[...] = h.astype(o_ref.dtype)


def norm_mod(x, g, mod3, scale_idx, shift_idx, dm, out_dtype):
    m, d = x.shape
    tm = dm.tm
    tpb = dm.seq // tm
    return pl.pallas_call(
        _norm_mod_kernel,
        grid=(m // tm,),
        in_specs=[pl.BlockSpec((tm, d), lambda i: (i, 0)),
                  pl.BlockSpec((1, d), lambda i: (0, 0)),
                  pl.BlockSpec((None, 1, d), lambda i: (i // tpb, 0, scale_idx)),
                  pl.BlockSpec((None, 1, d), lambda i: (i // tpb, 0, shift_idx))],
        out_specs=pl.BlockSpec((tm, d), lambda i: (i, 0)),
        out_shape=jax.ShapeDtypeStruct((m, d), out_dtype),
        compiler_params=_cp("parallel"),
        name="norm_mod",
    )(x, g.reshape(1, d), mod3, mod3)


def _proj_kernel(a_ref, w_ref, gain_ref, grp_ref, o_ref, wbf_ref, *, n_qk):
    j = pl.program_id(0)

    @pl.when(pl.program_id(1) == 0)
    def _():
        wbf_ref[...] = w_ref[...].astype(BF16)

    acc = jnp.dot(a_ref[...], wbf_ref[...], preferred_element_type=F32)

    @pl.when(j < n_qk)
    def _():
        sq = acc * acc
        hi = sq.astype(BF16)
        lo = (sq - hi.astype(F32)).astype(BF16)
        ss = (jnp.dot(hi, grp_ref[...], preferred_element_type=F32)
              + jnp.dot(lo, grp_ref[...], preferred_element_type=F32))
        o_ref[...] = (acc * lax.rsqrt(ss * (1.0 / DA_DQK) + EPS) * gain_ref[...]).astype(BF16)

    @pl.when(j >= n_qk)
    def _():
        o_ref[...] = acc.astype(BF16)


def proj_main(h, w_in, layer, qk_gain, dm):
    m, k = h.shape
    tm, tn = dm.tm, dm.tn
    n_main = dm.n_main
    n_qk = (2 * dm.d_branch) // tn
    grp = (np.arange(tn)[:, None] // DA_DQK == np.arange(tn)[None, :] // DA_DQK)
    grp = jnp.asarray(grp, BF16)
    return pl.pallas_call(
        functools.partial(_proj_kernel, n_qk=n_qk),
        grid=(n_main // tn, m // tm),
        in_specs=[pl.BlockSpec((tm, k), lambda j, i: (i, 0)),
                  pl.BlockSpec((None, k, tn), lambda j, i: (layer, 0, j)),
                  pl.BlockSpec((1, tn), lambda j, i: (0, jnp.minimum(j, n_qk - 1))),
                  pl.BlockSpec((tn, tn), lambda j, i: (0, 0))],
        out_specs=pl.BlockSpec((tm, tn), lambda j, i: (i, j)),
        out_shape=jax.ShapeDtypeStruct((m, n_main), BF16),
        scratch_shapes=[pltpu.VMEM((k, tn), BF16)],
        compiler_params=_cp("arbitrary", "arbitrary"),
        name="proj_main",
    )(h, w_in, qk_gain, grp)


def _mm_act_kernel(a_ref, w_ref, o_ref, wbf_ref, *, act):
    @pl.when(pl.program_id(1) == 0)
    def _():
        wbf_ref[...] = w_ref[...].astype(BF16)

    acc = jnp.dot(a_ref[...], wbf_ref[...], preferred_element_type=F32)
    if act == "sigmoid":
        acc = _sigmoid(acc)
    o_ref[...] = acc.astype(o_ref.dtype)


def mm_act(a, w3, layer, act, out_dtype, dm):
    m, k = a.shape
    n = w3.shape[-1]
    tm, tn = dm.tm, dm.tn
    return pl.pallas_call(
        functools.partial(_mm_act_kernel, act=act),
        grid=(n // tn, m // tm),
        in_specs=[pl.BlockSpec((tm, k), lambda j, i: (i, 0)),
                  pl.BlockSpec((None, k, tn), lambda j, i: (layer, 0, j))],
        out_specs=pl.BlockSpec((tm, tn), lambda j, i: (i, j)),
        out_shape=jax.ShapeDtypeStruct((m, n), out_dtype),
        scratch_shapes=[pltpu.VMEM((k, tn), BF16)],
        compiler_params=_cp("arbitrary", "arbitrary"),
        name="mm_act",
    )(a, w3)


def _ifgate_kernel(a_ref, w_ref, wt_ref, bc_ref, br_ref, g_ref, gt_ref, *, hm):
    a = a_ref[...]
    gc = jnp.dot(a, w_ref[...].astype(BF16), preferred_element_type=F32) + bc_ref[...]
    lane = lax.broadcasted_iota(jnp.int32, gc.shape, 1)
    g_ref[...] = jnp.where((lane >= hm) & (lane < 2 * hm), _log_sigmoid(gc), gc)
    gt = lax.dot_general(wt_ref[...].astype(BF16), a, (((1,), (1,)), ((), ())),
                         preferred_element_type=F32) + br_ref[...]
    row = lax.broadcasted_iota(jnp.int32, gt.shape, 0)
    gt_ref[...] = jnp.where((row >= hm) & (row < 2 * hm), _log_sigmoid(gt), gt)


def if_gates(h, w_in, layer, i_bias, f_bias, dm):
    m, k = h.shape
    hm = dm.ml_heads
    tm = dm.tm
    tpb = dm.seq // tm
    c0 = dm.n_main
    wt = jnp.zeros((SUBLANES, k), F32).at[:2 * hm].set(w_in[layer, :, c0:c0 + 2 * hm].T)
    bias = jnp.concatenate([i_bias, f_bias]).astype(F32)
    bc = jnp.zeros((1, LANES), F32).at[0, :2 * hm].set(bias)
    br = jnp.zeros((SUBLANES, 1), F32).at[:2 * hm, 0].set(bias)
    return pl.pallas_call(
        functools.partial(_ifgate_kernel, hm=hm),
        grid=(m // tm,),
        in_specs=[pl.BlockSpec((tm, k), lambda i: (i, 0)),
                  pl.BlockSpec((None, k, LANES), lambda i: (layer, 0, c0 // LANES)),
                  pl.BlockSpec((SUBLANES, k), lambda i: (0, 0)),
                  pl.BlockSpec((1, LANES), lambda i: (0, 0)),
                  pl.BlockSpec((SUBLANES, 1), lambda i: (0, 0))],
        out_specs=[pl.BlockSpec((tm, LANES), lambda i: (i, 0)),
                   pl.BlockSpec((None, SUBLANES, tm), lambda i: (i // tpb, 0, i % tpb))],
        out_shape=[jax.ShapeDtypeStruct((m, LANES), F32),
                   jax.ShapeDtypeStruct((dm.batch, SUBLANES, dm.seq), F32)],
        compiler_params=_cp("parallel"),
        name="if_gates",
    )(h, w_in, wt, bc, br)


def _t5_bucket_np(rel):
    n = jnp.maximum(rel, 0)
    max_exact = REL_BUCKETS // 2
    large = max_exact + (jnp.log(jnp.maximum(n, 1).astype(F32) / max_exact)
                         / math.log(REL_MAX_DIST / max_exact)
                         * (REL_BUCKETS - max_exact)).astype(jnp.int32)
    large = jnp.minimum(large, REL_BUCKETS - 1)
    return jnp.where(n < max_exact, n, large)


def _bias_tile_kernel(tab_ref, idd_ref, ide_ref, d_ref, e_ref):
    h2 = pl.program_id(0)
    far = tab_ref[REL_BUCKETS - 1, h2]
    idd = idd_ref[...]
    ide = ide_ref[...]
    d = jnp.zeros(idd.shape, F32)
    e = jnp.zeros(ide.shape, F32)
    for b in range(REL_BUCKETS):
        t = tab_ref[b, h2] - far
        d = jnp.where(idd == b, t, d)
        e = jnp.where(ide == b, t, e)
    i = lax.broadcasted_iota(jnp.int32, idd.shape, 0)
    j = lax.broadcasted_iota(jnp.int32, idd.shape, 1)
    d_ref[...] = jnp.where(j <= i, d, NEG)
    e_ref[...] = e


def bias_tiles(rel_bias):
    nb, h2 = rel_bias.shape
    i = jnp.arange(LANES, dtype=jnp.int32)[:, None]
    j = jnp.arange(LANES, dtype=jnp.int32)[None, :]
    idd = _t5_bucket_np(i - j)
    ide = _t5_bucket_np(LANES + i - j)
    spec = pl.BlockSpec((LANES, LANES), lambda h: (0, 0))
    ospec = pl.BlockSpec((None, LANES, LANES), lambda h: (h, 0, 0))
    return pl.pallas_call(
        _bias_tile_kernel,
        grid=(h2,),
        in_specs=[pl.BlockSpec(memory_space=pltpu.SMEM), spec, spec],
        out_specs=[ospec, ospec],
        out_shape=[jax.ShapeDtypeStruct((h2, LANES, LANES), F32)] * 2,
        compiler_params=_cp("arbitrary"),
        name="bias_tiles",
    )(rel_bias.astype(F32), idd, ide)


def _attn_kernel(q_ref, k_ref, v_ref, d0_ref, e0_ref, lam_ref, subg_ref, o_ref,
                 qq_ref, m_ref, l_ref, acc_ref, bd_ref, be_ref, *, tq, lam_init):
    qi = pl.program_id(2)
    nsub = tq // LANES

    @pl.when(qi == 0)
    def _():
        for mp in range(2):
            for a in range(nsub):
                for b in range(nsub):
                    rows = pl.ds(mp * tq + a * LANES, LANES)
                    cols = pl.ds(b * LANES, LANES)
                    if a == b:
                        bd_ref[rows, cols] = d0_ref[mp]
                    elif a == b + 1:
                        bd_ref[rows, cols] = e0_ref[mp]
                    elif a < b:
                        bd_ref[rows, cols] = jnp.full((LANES, LANES), NEG, F32)
                    else:
                        bd_ref[rows, cols] = jnp.zeros((LANES, LANES), F32)
                    if a == 0 and b == nsub - 1:
                        be_ref[rows, cols] = e0_ref[mp]
                    else:
                        be_ref[rows, cols] = jnp.zeros((LANES, LANES), F32)

    q = q_ref[...]
    lane = lax.broadcasted_iota(jnp.int32, q.shape, 1)
    zero = jnp.zeros_like(q)
    qq_ref[pl.ds(0, tq), :] = jnp.where(lane < DA_DQK, q, zero)
    qq_ref[pl.ds(tq, tq), :] = jnp.where(lane >= DA_DQK, q, zero)
    m_ref[...] = jnp.full(m_ref.shape, NEG, F32)
    l_ref[...] = jnp.zeros(l_ref.shape, F32)
    acc_ref[...] = jnp.zeros(acc_ref.shape, F32)

    def step(start, bias_ref):
        k = k_ref[pl.ds(start, tq), :]
        v = v_ref[pl.ds(start, tq), :]
        s = lax.dot_general(qq_ref[...], k, (((1,), (1,)), ((), ())),
                            preferred_element_type=F32)
        if bias_ref is not None:
            s = s + bias_ref[...]
        m_prev = m_ref[...]
        m_new = jnp.maximum(m_prev, jnp.max(s, axis=1, keepdims=True))
        alpha = jnp.exp(m_prev - m_new)
        p = jnp.exp(s - m_new)
        l_ref[...] = alpha * l_ref[...] + jnp.sum(p, axis=1, keepdims=True)
        acc_ref[...] = alpha * acc_ref[...] + jnp.dot(p.astype(BF16), v,
                                                      preferred_element_type=F32)
        m_ref[...] = m_new

    def far_body(ki, carry):
        step(pl.multiple_of(ki * tq, tq), None)
        return carry

    lax.fori_loop(0, jnp.maximum(qi - 1, 0), far_body, 0)

    @pl.when(qi >= 1)
    def _():
        step(pl.multiple_of((qi - 1) * tq, tq), be_ref)

    step(pl.multiple_of(qi * tq, tq), bd_ref)

    o = acc_ref[...] / l_ref[...]
    lamv = lam_ref[...]
    lam = (jnp.exp(jnp.sum(lamv[0:1] * lamv[1:2], axis=1, keepdims=True))
           - jnp.exp(jnp.sum(lamv[2:3] * lamv[3:4], axis=1, keepdims=True)) + lam_init)
    od = o[:tq] - lam * o[tq:]
    ms = jnp.mean(od * od, axis=-1, keepdims=True)
    y = od * lax.rsqrt(ms + EPS) * subg_ref[...] * (1.0 - lam_init)
    o_ref[...] = y.astype(o_ref.dtype)


def diff_attention(proj, d0, e0, lamv, sub_g, lam_init, dm):
    m = proj.shape[0]
    t, tq, h = dm.seq, dm.tq, dm.da_heads
    nq = t // tq
    return pl.pallas_call(
        functools.partial(_attn_kernel, tq=tq, lam_init=lam_init),
        grid=(dm.batch, h, nq),
        in_specs=[pl.BlockSpec((tq, LANES), lambda b, hh, qi: (b * nq + qi, hh)),
                  pl.BlockSpec((t, LANES), lambda b, hh, qi: (b, h + hh)),
                  pl.BlockSpec((t, LANES), lambda b, hh, qi: (b, 2 * h + hh)),
                  pl.BlockSpec((2, LANES, LANES), lambda b, hh, qi: (hh, 0, 0)),
                  pl.BlockSpec((2, LANES, LANES), lambda b, hh, qi: (hh, 0, 0)),
                  pl.BlockSpec((4, DA_DQK), lambda b, hh, qi: (0, 0)),
                  pl.BlockSpec((1, DA_DV), lambda b, hh, qi: (0, 0))],
        out_specs=pl.BlockSpec((tq, LANES), lambda b, hh, qi: (b * nq + qi, hh)),
        out_shape=jax.ShapeDtypeStruct((m, dm.d_branch), BF16),
        scratch_shapes=[pltpu.VMEM((2 * tq, LANES), BF16),
                        pltpu.VMEM((2 * tq, 1), F32),
                        pltpu.VMEM((2 * tq, 1), F32),
                        pltpu.VMEM((2 * tq, LANES), F32),
                        pltpu.VMEM((2 * tq, tq), F32),
                        pltpu.VMEM((2 * tq, tq), F32)],
        compiler_params=_cp("parallel", "parallel", "arbitrary"),
        name="diff_attention",
    )(proj, proj, proj, d0, e0, lamv, sub_g.reshape(1, DA_DV))


def _mlstm_kernel(q_ref, k_ref, v_ref, og_ref, g_ref, gt_ref, cwq_ref, cwk_ref,
                  cbq_ref, cbk_ref, ng_ref, o_ref,
                  qbuf_ref, kbuf_ref, ct_ref, n_ref, m_ref, *, L, hm):
    h = pl.program_id(1)
    c = pl.program_id(2)

    @pl.when(c == 0)
    def _():
        qbuf_ref[pl.ds(0, SUBLANES), :] = jnp.zeros((SUBLANES, ML_DQK), F32)
        kbuf_ref[pl.ds(0, SUBLANES), :] = jnp.zeros((SUBLANES, ML_DQK), F32)
        ct_ref[...] = jnp.zeros(ct_ref.shape, F32)
        n_ref[...] = jnp.zeros(n_ref.shape, F32)
        m_ref[...] = jnp.zeros(m_ref.shape, F32)

    def conv_silu(x_ref, buf_ref, w_ref, b_ref):
        buf_ref[pl.ds(SUBLANES, L), :] = x_ref[...].astype(F32)
        w = w_ref[...]
        y = b_ref[...] + w[CONV_W - 1:CONV_W] * buf_ref[pl.ds(SUBLANES, L), :]
        for s in range(1, CONV_W):
            y = y + w[CONV_W - 1 - s:CONV_W - s] * buf_ref[pl.ds(SUBLANES - s, L), :]
        tail = buf_ref[pl.ds(L, SUBLANES), :]
        buf_ref[pl.ds(0, SUBLANES), :] = tail
        return y * _sigmoid(y)

    q = conv_silu(q_ref, qbuf_ref, cwq_ref, cbq_ref)
    k = conv_silu(k_ref, kbuf_ref, cwk_ref, cbk_ref) * (ML_DQK ** -0.5)
    v = v_ref[...]

    g = g_ref[...]
    gt = gt_ref[...]
    ii = lax.broadcasted_iota(jnp.int32, (L, L), 0)
    jj = lax.broadcasted_iota(jnp.int32, (L, L), 1)
    tril = (jj <= ii).astype(F32)
    triu = (ii <= jj).astype(F32)
    bcum = jnp.dot(tril, g, preferred_element_type=F32, precision=lax.Precision.HIGHEST)
    btcum = jnp.dot(gt, triu, preferred_element_type=F32, precision=lax.Precision.HIGHEST)
    lane = lax.broadcasted_iota(jnp.int32, g.shape, 1)
    ig_col = jnp.sum(jnp.where(lane == h, g, 0.0), axis=1, keepdims=True)
    b_col = jnp.sum(jnp.where(lane == hm + h, bcum, 0.0), axis=1, keepdims=True)
    row = lax.broadcasted_iota(jnp.int32, gt.shape, 0)
    ig_row = jnp.sum(jnp.where(row == h, gt, 0.0), axis=0, keepdims=True)
    b_row = jnp.sum(jnp.where(row == hm + h, btcum, 0.0), axis=0, keepdims=True)

    m_old = m_ref[...]
    dlog = jnp.where(jj <= ii, b_col - b_row + ig_row, NEG)
    m_inter = b_col + m_old
    m_s = jnp.maximum(m_inter, jnp.max(dlog, axis=1, keepdims=True))
    dmat = jnp.exp(dlog - m_s)
    qb = q.astype(BF16)
    kb = k.astype(BF16)
    qk = lax.dot_general(qb, kb, (((1,), (1,)), ((), ())), preferred_element_type=F32)
    w = dmat * qk
    inter = jnp.exp(m_inter - m_s)
    ct = ct_ref[...]
    n_row = n_ref[...]
    num = (jnp.dot(w.astype(BF16), v, preferred_element_type=F32)
           + inter * jnp.dot(qb, ct.astype(BF16), preferred_element_type=F32))
    den = (jnp.sum(w, axis=1, keepdims=True)
           + inter * jnp.sum(q * n_row, axis=1, keepdims=True))
    hh = num / jnp.maximum(jnp.abs(den), jnp.exp(-m_s))

    m_new = m_s[L - 1:L]
    b_last = b_col[L - 1:L]
    w_end = jnp.exp(b_last - b_col + ig_col - m_new)
    decay = jnp.exp(b_last + m_old - m_new)
    kw = k * w_end
    ct_ref[...] = decay * ct + lax.dot_general(kw.astype(BF16), v, (((0,), (0,)), ((), ())),
                                               preferred_element_type=F32)
    n_ref[...] = decay * n_row + jnp.sum(kw, axis=0, keepdims=True)
    m_ref[...] = m_new

    ms = jnp.mean(hh * hh, axis=-1, keepdims=True)
    y = hh * lax.rsqrt(ms + EPS) * ng_ref[...] * _sigmoid(og_ref[...].astype(F32))
    o_ref[...] = y.astype(o_ref.dtype)


def mlstm_branch(proj, g, gt, conv_w, conv_b, norm_g, dm):
    m = proj.shape[0]
    t, L, hm, h = dm.seq, dm.ml_chunk, dm.ml_heads, dm.da_heads
    nc = t // L
    qc0 = 3 * h
    kc0 = 3 * h + hm
    vc0 = 4 * hm
    oc0 = 5 * hm
    cw = conv_w.astype(F32)
    cb = conv_b.astype(F32).reshape(1, -1)
    return pl.pallas_call(
        functools.partial(_mlstm_kernel, L=L, hm=hm),
        grid=(dm.batch, hm, nc),
        in_specs=[pl.BlockSpec((L, ML_DQK), lambda b, hh, c: (b * nc + c, qc0 + hh)),
                  pl.BlockSpec((L, ML_DQK), lambda b, hh, c: (b * nc + c, kc0 + hh)),
                  pl.BlockSpec((L, ML_DV), lambda b, hh, c: (b * nc + c, vc0 + hh)),
                  pl.BlockSpec((L, ML_DV), lambda b, hh, c: (b * nc + c, oc0 + hh)),
                  pl.BlockSpec((L, LANES), lambda b, hh, c: (b * nc + c, 0)),
                  pl.BlockSpec((None, SUBLANES, L), lambda b, hh, c: (b, 0, c)),
                  pl.BlockSpec((CONV_W, ML_DQK), lambda b, hh, c: (0, hh)),
                  pl.BlockSpec((CONV_W, ML_DQK), lambda b, hh, c: (0, hm + hh)),
                  pl.BlockSpec((1, ML_DQK), lambda b, hh, c: (0, hh)),
                  pl.BlockSpec((1, ML_DQK), lambda b, hh, c: (0, hm + hh)),
                  pl.BlockSpec((1, ML_DV), lambda b, hh, c: (0, hh))],
        out_specs=pl.BlockSpec((L, ML_DV), lambda b, hh, c: (b * nc + c, hh)),
        out_shape=jax.ShapeDtypeStruct((m, dm.d_branch), BF16),
        scratch_shapes=[pltpu.VMEM((L + SUBLANES, ML_DQK), F32),
                        pltpu.VMEM((L + SUBLANES, ML_DQK), F32),
                        pltpu.VMEM((ML_DQK, ML_DV), F32),
                        pltpu.VMEM((1, ML_DQK), F32),
                        pltpu.VMEM((1, 1), F32)],
        compiler_params=_cp("parallel", "parallel", "arbitrary"),
        name="mlstm_branch",
    )(proj, proj, proj, proj, g, gt, cw, cw, cb, cb, norm_g.reshape(1, -1))


def _merge_kernel(ya_ref, yb_ref, wa_ref, wb_ref, sa_ref, sb_ref, o_ref, wabf_ref, wbbf_ref):
    @pl.when(pl.program_id(1) == 0)
    def _():
        wabf_ref[...] = wa_ref[...].astype(BF16)
        wbbf_ref[...] = wb_ref[...].astype(BF16)

    pa = jnp.dot(ya_ref[...], wabf_ref[...], preferred_element_type=F32)
    pb = jnp.dot(yb_ref[...], wbbf_ref[...], preferred_element_type=F32)
    o_ref[...] = (sa_ref[...].astype(F32) * pa + sb_ref[...].astype(F32) * pb).astype(o_ref.dtype)


def merge_branches(ya, yb, wa, wb, layer, sg, dm):
    m, kb = ya.shape
    d = dm.d_model
    tm, tn = dm.tm, dm.tn
    nj = d // tn
    return pl.pallas_call(
        _merge_kernel,
        grid=(nj, m // tm),
        in_specs=[pl.BlockSpec((tm, kb), lambda j, i: (i, 0)),
                  pl.BlockSpec((tm, kb), lambda j, i: (i, 0)),
                  pl.BlockSpec((None, kb, tn), lambda j, i: (layer, 0, j)),
                  pl.BlockSpec((None, kb, tn), lambda j, i: (layer, 0, j)),
                  pl.BlockSpec((tm, tn), lambda j, i: (i, j)),
                  pl.BlockSpec((tm, tn), lambda j, i: (i, nj + j))],
        out_specs=pl.BlockSpec((tm, tn), lambda j, i: (i, j)),
        out_shape=jax.ShapeDtypeStruct((m, d), BF16),
        scratch_shapes=[pltpu.VMEM((kb, tn), BF16), pltpu.VMEM((kb, tn), BF16)],
        compiler_params=_cp("arbitrary", "arbitrary"),
        name="merge_branches",
    )(ya, yb, wa, wb, sg, sg)


def _mm_res_kernel(a_ref, w_ref, x_ref, gate_ref, o_ref, wbf_ref):
    @pl.when(pl.program_id(1) == 0)
    def _():
        wbf_ref[...] = w_ref[...].astype(BF16)

    acc = jnp.dot(a_ref[...], wbf_ref[...], preferred_element_type=F32)
    o_ref[...] = x_ref[...] + gate_ref[...] * acc


def mm_residual(a, w3, layer, x, mod3, gate_idx, dm, tm=None):
    m, k = a.shape
    d = x.shape[1]
    tm = tm or dm.tm
    tn = dm.tn
    nj = d // tn
    tpb = dm.seq // tm
    return pl.pallas_call(
        _mm_res_kernel,
        grid=(nj, m // tm),
        in_specs=[pl.BlockSpec((tm, k), lambda j, i: (i, 0)),
                  pl.BlockSpec((None, k, tn), lambda j, i: (layer, 0, j)),
                  pl.BlockSpec((tm, tn), lambda j, i: (i, j)),
                  pl.BlockSpec((None, 1, tn), lambda j, i: (i // tpb, 0, gate_idx * nj + j))],
        out_specs=pl.BlockSpec((tm, tn), lambda j, i: (i, j)),
        out_shape=jax.ShapeDtypeStruct((m, d), F32),
        scratch_shapes=[pltpu.VMEM((k, tn), w3.dtype if w3.dtype == BF16 else BF16)],
        compiler_params=_cp("arbitrary", "arbitrary"),
        name="mm_residual",
    )(a, w3, x, mod3)


def _swiglu(a, w1, w3):
    p1 = jnp.dot(a, w1, preferred_element_type=F32)
    p3 = jnp.dot(a, w3, preferred_element_type=F32)
    return (p1 * _sigmoid(p1) * p3).astype(BF16)


def _up_kernel(a_ref, w1_ref, w3_ref, o_ref, w1bf_ref, w3bf_ref):
    @pl.when(pl.program_id(1) == 0)
    def _():
        w1bf_ref[...] = w1_ref[...].astype(BF16)
        w3bf_ref[...] = w3_ref[...].astype(BF16)

    o_ref[...] = _swiglu(a_ref[...], w1bf_ref[...], w3bf_ref[...])


def ffn_up(a, w1, w3, layer, dm):
    m, k = a.shape
    f = w1.shape[-1]
    tm, tn = dm.tm, dm.tn
    return pl.pallas_call(
        _up_kernel,
        grid=(f // tn, m // tm),
        in_specs=[pl.BlockSpec((tm, k), lambda j, i: (i, 0)),
                  pl.BlockSpec((None, k, tn), lambda j, i: (layer, 0, j)),
                  pl.BlockSpec((None, k, tn), lambda j, i: (layer, 0, j))],
        out_specs=pl.BlockSpec((tm, tn), lambda j, i: (i, j)),
        out_shape=jax.ShapeDtypeStruct((m, f), BF16),
        scratch_shapes=[pltpu.VMEM((k, tn), BF16), pltpu.VMEM((k, tn), BF16)],
        compiler_params=_cp("arbitrary", "arbitrary"),
        name="ffn_up",
    )(a, w1, w3)


def _gup_kernel(te_ref, tv_ref, a_ref, w1_ref, w3_ref, o_ref, w1bf_ref, w3bf_ref):
    i = pl.program_id(1)
    changed = (i == 0) | (te_ref[i] != te_ref[jnp.maximum(i - 1, 0)])

    @pl.when(changed)
    def _():
        w1bf_ref[...] = w1_ref[...].astype(BF16)
        w3bf_ref[...] = w3_ref[...].astype(BF16)

    @pl.when(tv_ref[i] == 1)
    def _():
        o_ref[...] = _swiglu(a_ref[...], w1bf_ref[...], w3bf_ref[...])

    @pl.when(tv_ref[i] == 0)
    def _():
        o_ref[...] = jnp.zeros(o_ref.shape, o_ref.dtype)


def moe_up(hs, w1, w3, lm, te, tv, dm):
    p, k = hs.shape
    f = w1.shape[-1]
    tg, tn = dm.tg, dm.tn
    grid_spec = pltpu.PrefetchScalarGridSpec(
        num_scalar_prefetch=2,
        grid=(f // tn, p // tg),
        in_specs=[pl.BlockSpec((tg, k), lambda j, i, te, tv: (i, 0)),
                  pl.BlockSpec((None, None, k, tn), lambda j, i, te, tv: (lm, te[i], 0, j)),
                  pl.BlockSpec((None, None, k, tn), lambda j, i, te, tv: (lm, te[i], 0, j))],
        out_specs=pl.BlockSpec((tg, tn), lambda j, i, te, tv: (i, j)),
        scratch_shapes=[pltpu.VMEM((k, tn), BF16), pltpu.VMEM((k, tn), BF16)])
    return pl.pallas_call(
        _gup_kernel,
        grid_spec=grid_spec,
        out_shape=jax.ShapeDtypeStruct((p, f), BF16),
        compiler_params=_cp("arbitrary", "arbitrary"),
        name="moe_up",
    )(te, tv, hs, w1, w3)


def _gdown_kernel(te_ref, tv_ref, u_ref, w_ref, o_ref):
    i = pl.program_id(1)

    @pl.when(tv_ref[i] == 1)
    def _():
        o_ref[...] = jnp.dot(u_ref[...], w_ref[...], preferred_element_type=F32)

    @pl.when(tv_ref[i] == 0)
    def _():
        o_ref[...] = jnp.zeros(o_ref.shape, o_ref.dtype)


def moe_down(u, w2bf, te, tv, dm):
    p, f = u.shape
    d = w2bf.shape[-1]
    tg, tn = dm.tg, dm.tn
    grid_spec = pltpu.PrefetchScalarGridSpec(
        num_scalar_prefetch=2,
        grid=(d // tn, p // tg),
        in_specs=[pl.BlockSpec((tg, f), lambda j, i, te, tv: (i, 0)),
                  pl.BlockSpec((None, f, tn), lambda j, i, te, tv: (te[i], 0, j))],
        out_specs=pl.BlockSpec((tg, tn), lambda j, i, te, tv: (i, j)))
    return pl.pallas_call(
        _gdown_kernel,
        grid_spec=grid_spec,
        out_shape=jax.ShapeDtypeStruct((p, d), F32),
        compiler_params=_cp("arbitrary", "arbitrary"),
        name="moe_down",
    )(te, tv, u, w2bf)


def _router_kernel(a_ref, w_ref, o_ref, *, n_experts):
    logits = jnp.dot(a_ref[...].astype(BF16), w_ref[...].astype(BF16),
                     preferred_element_type=F32)
    lane = lax.broadcasted_iota(jnp.int32, logits.shape, 1)
    lg = jnp.where(lane < n_experts, logits, NEG)
    v1 = jnp.max(lg, axis=1, keepdims=True)
    i1 = jnp.min(jnp.where(lg == v1, lane, LANES), axis=1, keepdims=True)
    lg2 = jnp.where(lane == i1, NEG, lg)
    v2 = jnp.max(lg2, axis=1, keepdims=True)
    i2 = jnp.min(jnp.where(lg2 == v2, lane, LANES), axis=1, keepdims=True)
    e2 = jnp.exp(v2 - v1)
    g1 = 1.0 / (1.0 + e2)
    g2 = e2 / (1.0 + e2)
    out = jnp.where(lane == 0, g1, 0.0)
    out = jnp.where(lane == 1, g2, out)
    out = jnp.where(lane == 2, i1.astype(F32), out)
    out = jnp.where(lane == 3, i2.astype(F32), out)
    o_ref[...] = out


def router(hf, router_w, lm, dm):
    m, k = hf.shape
    e = dm.n_experts
    tm = dm.tm
    wpad = jnp.zeros((k, LANES), F32).at[:, :e].set(router_w[lm])
    return pl.pallas_call(
        functools.partial(_router_kernel, n_experts=e),
        grid=(m // tm,),
        in_specs=[pl.BlockSpec((tm, k), lambda i: (i, 0)),
                  pl.BlockSpec((k, LANES), lambda i: (0, 0))],
        out_specs=pl.BlockSpec((tm, LANES), lambda i: (i, 0)),
        out_shape=jax.ShapeDtypeStruct((m, LANES), F32),
        compiler_params=_cp("parallel"),
        name="router",
    )(hf, wpad)


def _row_copy(src_ref, dst_ref, src_row, dst_row, sem):
    return pltpu.make_async_copy(src_ref.at[pl.ds(src_row, 1), :],
                                 dst_ref.at[pl.ds(dst_row, 1), :], sem)


def _gather_kernel(idx_ref, src_ref, o_ref, buf_ref, sem, *, rows):
    def issue(r, carry):
        _row_copy(src_ref, buf_ref, idx_ref[0, 0, r], r, sem).start()
        return carry

    lax.fori_loop(0, rows, issue, 0)

    def wait(r, carry):
        _row_copy(src_ref, buf_ref, 0, r, sem).wait()
        return carry

    lax.fori_loop(0, rows, wait, 0)
    o_ref[...] = buf_ref[...].astype(o_ref.dtype)


def gather_rows(src, idx, rows, out_dtype):
    p = idx.shape[0]
    d = src.shape[1]
    nblk = p // rows
    return pl.pallas_call(
        functools.partial(_gather_kernel, rows=rows),
        grid=(nblk,),
        in_specs=[pl.BlockSpec((1, 1, rows), lambda i: (i, 0, 0), memory_space=pltpu.SMEM),
                  pl.BlockSpec(memory_space=pl.ANY)],
        out_specs=pl.BlockSpec((rows, d), lambda i: (i, 0)),
        out_shape=jax.ShapeDtypeStruct((p, d), out_dtype),
        scratch_shapes=[pltpu.VMEM((rows, d), src.dtype), pltpu.SemaphoreType.DMA(())],
        compiler_params=_cp("arbitrary"),
        name="gather_rows",
    )(idx.reshape(nblk, 1, rows), src)


def _combine_kernel(pos_ref, ys_ref, x_ref, r_ref, gate_ref, o_ref, buf_ref, sem, *, rows):
    def issue(r, carry):
        _row_copy(ys_ref, buf_ref.at[0], pos_ref[0, 0, 2 * r], r, sem).start()
        _row_copy(ys_ref, buf_ref.at[1], pos_ref[0, 0, 2 * r + 1], r, sem).start()
        return carry

    lax.fori_loop(0, rows, issue, 0)

    def wait(r, carry):
        _row_copy(ys_ref, buf_ref.at[0], 0, r, sem).wait()
        _row_copy(ys_ref, buf_ref.at[1], 0, r, sem).wait()
        return carry

    lax.fori_loop(0, rows, wait, 0)
    rr = r_ref[...]
    mix = rr[:, 0:1] * buf_ref[0] + rr[:, 1:2] * buf_ref[1]
    o_ref[...] = x_ref[...] + gate_ref[...] * mix


def moe_combine(ys, pos, x, rinfo, mod3, gate_idx, dm):
    m, d = x.shape
    rows = dm.tc
    nblk = m // rows
    tpb = dm.seq // rows
    return pl.pallas_call(
        functools.partial(_combine_kernel, rows=rows),
        grid=(nblk,),
        in_specs=[pl.BlockSpec((1, 1, 2 * rows), lambda i: (i, 0, 0), memory_space=pltpu.SMEM),
                  pl.BlockSpec(memory_space=pl.ANY),
                  pl.BlockSpec((rows, d), lambda i: (i, 0)),
                  pl.BlockSpec((rows, LANES), lambda i: (i, 0)),
                  pl.BlockSpec((None, 1, d), lambda i: (i // tpb, 0, gate_idx))],
        out_specs=pl.BlockSpec((rows, d), lambda i: (i, 0)),
        out_shape=jax.ShapeDtypeStruct((m, d), F32),
        scratch_shapes=[pltpu.VMEM((2, rows, d), F32), pltpu.SemaphoreType.DMA(())],
        compiler_params=_cp("arbitrary"),
        name="moe_combine",
    )(pos.reshape(nblk, 1, 2 * rows), ys, x, rinfo, mod3)


def routing_tables(rinfo, dm):
    m = rinfo.shape[0]
    e, tg = dm.n_experts, dm.tg
    n_tiles = (TOP_K * m) // tg + e
    eid = rinfo[:, 2:2 + TOP_K].astype(jnp.int32).reshape(-1)
    onehot = (eid[:, None] == jnp.arange(e, dtype=jnp.int32)[None, :]).astype(jnp.int32)
    csum = jnp.cumsum(onehot, axis=0)
    rank = jnp.sum((csum - onehot) * onehot, axis=1)
    counts = csum[-1]
    tiles_per = (counts + tg - 1) // tg
    tile_end = jnp.cumsum(tiles_per)
    tile_start = tile_end - tiles_per
    pos = tile_start[eid] * tg + rank
    token = jnp.arange(TOP_K * m, dtype=jnp.int32) // TOP_K
    slot_token = jnp.zeros((n_tiles * tg,), jnp.int32).at[pos].set(token)
    tidx = jnp.arange(n_tiles, dtype=jnp.int32)
    used = tile_end[-1]
    te_raw = jnp.sum((tidx[:, None] >= tile_end[None, :]).astype(jnp.int32), axis=1)
    last_e = jnp.sum((jnp.maximum(used - 1, 0) >= tile_end).astype(jnp.int32))
    te = jnp.where(tidx < used, te_raw, last_e).astype(jnp.int32)
    tv = (tidx < used).astype(jnp.int32)
    return pos.astype(jnp.int32), slot_token, te, tv


def forward(dm, x, c, rel_bias, w_mod, b_mod, norm_mix_g, norm_ff_g, w_in,
            da_q_g, da_k_g, da_lam_q1, da_lam_k1, da_lam_q2, da_lam_k2, da_sub_g,
            ml_conv_w, ml_conv_b, ml_i_bias, ml_f_bias, ml_norm_g,
            w_branch_a, w_branch_b, w_out,
            ffn_w1, ffn_w3, ffn_w2, router_w, moe_w1, moe_w3, moe_w2):
    b, t, d = x.shape
    m = b * t
    h = dm.da_heads
    xf = x.astype(F32).reshape(m, d)
    mod_all = adaln_mod(c, w_mod, b_mod)
    d0, e0 = bias_tiles(rel_bias)
    c_gate = dm.n_main + 2 * dm.ml_heads
    w_g = w_in[:, :, c_gate:]
    ffn_w2_bf = ffn_w2.astype(BF16)
    moe_w2_bf = moe_w2.astype(BF16)

    for l in range(dm.depth):
        mod3 = mod_all[l].reshape(b, 1, 6 * d)
        hb = norm_mod(xf, norm_mix_g[l], mod3, 1, 0, dm, BF16)
        qk_gain = jnp.concatenate([jnp.tile(da_q_g[l].astype(F32), 2 * h) * (DA_DQK ** -0.5),
                                   jnp.tile(da_k_g[l].astype(F32), 2 * h)]).reshape(1, -1)
        proj = proj_main(hb, w_in, l, qk_gain, dm)
        sg = mm_act(hb, w_g, l, "sigmoid", BF16, dm)
        g, gt = if_gates(hb, w_in, l, ml_i_bias[l], ml_f_bias[l], dm)
        lam_init = 0.8 - 0.6 * math.exp(-0.3 * l)
        lamv = jnp.stack([da_lam_q1[l], da_lam_k1[l], da_lam_q2[l], da_lam_k2[l]]).astype(F32)
        ya = diff_attention(proj, d0, e0, lamv, da_sub_g[l].astype(F32), lam_init, dm)
        yb = mlstm_branch(proj, g, gt, ml_conv_w[l], ml_conv_b[l], ml_norm_g[l].astype(F32), dm)
        merged = merge_branches(ya, yb, w_branch_a, w_branch_b, l, sg, dm)
        xf = mm_residual(merged, w_out, l, xf, mod3, 2, dm)
        if l % 2 == 0:
            hb = norm_mod(xf, norm_ff_g[l], mod3, 4, 3, dm, BF16)
            u = ffn_up(hb, ffn_w1, ffn_w3, l // 2, dm)
            xf = mm_residual(u, ffn_w2_bf, l // 2, xf, mod3, 5, dm)
        else:
            lm = l // 2
            hf = norm_mod(xf, norm_ff_g[l], mod3, 4, 3, dm, F32)
            rinfo = router(hf, router_w, lm, dm)
            pos, slot_token, te, tv = routing_tables(rinfo, dm)
            hs = gather_rows(hf, slot_token, dm.tg, BF16)
            u = moe_up(hs, moe_w1, moe_w3, lm, te, tv, dm)
            ys = moe_down(u, moe_w2_bf[lm], te, tv, dm)
            xf = moe_combine(ys, pos, xf, rinfo, mod3, 5, dm)
    return xf.reshape(b, t, d)


def kernel(x, c, rel_bias, w_mod, b_mod, norm_mix_g, norm_ff_g, w_in, da_q_g, da_k_g, da_lam_q1, da_lam_k1, da_lam_q2, da_lam_k2, da_sub_g, ml_conv_w, ml_conv_b, ml_i_bias, ml_f_bias, ml_norm_g, w_branch_a, w_branch_b, w_out, ffn_w1, ffn_w3, ffn_w2, router_w, moe_w1, moe_w3, moe_w2):
    b, t, d = x.shape
    dm = Dims(d_model=d, batch=b, seq=t, depth=w_mod.shape[0], d_ff=ffn_w1.shape[-1],
              n_experts=router_w.shape[-1])
    return forward(dm, x, c, rel_bias, w_mod, b_mod, norm_mix_g, norm_ff_g, w_in,
                   da_q_g, da_k_g, da_lam_q1, da_lam_k1, da_lam_q2, da_lam_k2, da_sub_g,
                   ml_conv_w, ml_conv_b, ml_i_bias, ml_f_bias, ml_norm_g,
                   w_branch_a, w_branch_b, w_out,
                   ffn_w1, ffn_w3, ffn_w2, router_w, moe_w1, moe_w3, moe_w2)
```

```python
import functools
import math
from typing import NamedTuple

import numpy as np
import jax
import jax.numpy as jnp
from jax import lax
from jax.experimental import pallas as pl
from jax.experimental.pallas import tpu as pltpu

F32 = jnp.float32
BF16 = jnp.bfloat16

DA_DQK = 64
DA_DV = 128
ML_DQK = 128
ML_DV = 256
CONV_W = 4
TOP_K = 2
REL_BUCKETS = 32
REL_MAX_DIST = 128
EPS = 1e-6
NEG = -1e30
LOG2E = math.log2(math.e)
LANES = 128
SUBLANES = 8
VMEM_LIMIT = 52 * 1024 * 1024
DMA_UNROLL = 8


class Dims(NamedTuple):
    d_model: int
    batch: int
    seq: int
    depth: int
    d_ff: int
    n_experts: int
    tm: int = 512
    tm_dense: int = 1024
    tn: int = 512
    tn_wide: int = 1024
    tq: int = 512
    ml_chunk: int = 256
    tg: int = 512
    tc: int = 256
    attn_rc: int = 256

    @property
    def d_branch(self):
        return self.d_model // 2

    @property
    def da_heads(self):
        return self.d_branch // DA_DV

    @property
    def ml_heads(self):
        return self.d_branch // ML_DV

    @property
    def n_main(self):
        return 3 * self.d_branch + 2 * self.ml_heads * ML_DQK + 2 * self.d_branch

    @property
    def tokens(self):
        return self.batch * self.seq


def _cp(*sem):
    return pltpu.CompilerParams(dimension_semantics=sem, vmem_limit_bytes=VMEM_LIMIT)


def _sigmoid(x):
    return 1.0 / (1.0 + jnp.exp(-x))


def _log_sigmoid(x):
    return jnp.minimum(x, 0.0) - jnp.log(1.0 + jnp.exp(-jnp.abs(x)))


def _mod_kernel(c_ref, w_ref, b_ref, o_ref):
    c = c_ref[...]
    ca = c * _sigmoid(c)
    o_ref[...] = jnp.dot(ca, w_ref[...], preferred_element_type=F32,
                         precision=lax.Precision.HIGHEST) + b_ref[...]


def adaln_mod(c, w_mod, b_mod):
    depth, d, n = w_mod.shape
    b = c.shape[0]
    bp = -(-b // SUBLANES) * SUBLANES
    cp = jnp.zeros((bp, d), F32).at[:b].set(c.astype(F32))
    tn = min(n, 1024)
    out = pl.pallas_call(
        _mod_kernel,
        grid=(depth, n // tn),
        in_specs=[pl.BlockSpec((bp, d), lambda l, j: (0, 0)),
                  pl.BlockSpec((None, d, tn), lambda l, j: (l, 0, j)),
                  pl.BlockSpec((None, 1, tn), lambda l, j: (l, 0, j))],
        out_specs=pl.BlockSpec((None, bp, tn), lambda l, j: (l, 0, j)),
        out_shape=jax.ShapeDtypeStruct((depth, bp, n), F32),
        compiler_params=_cp("parallel", "parallel"),
        name="adaln_mod",
    )(cp, w_mod, b_mod.reshape(depth, 1, n))
    return out[:, :b]


def _norm_mod_kernel(x_ref, g_ref, sc_ref, sh_ref, o_ref):
    x = x_ref[...]
    ms = jnp.mean(x * x, axis=-1, keepdims=True)
    h = x * lax.rsqrt(ms + EPS) * g_ref[...] * (1.0 + sc_ref[...]) + sh_ref[...]
    o_ref[...] = h.astype(o_ref.dtype)


def norm_mod(x, g, mod3, scale_idx, shift_idx, dm, out_dtype):
    m, d = x.shape
    tm = dm.tm
    tpb = dm.seq // tm
    return pl.pallas_call(
        _norm_mod_kernel,
        grid=(m // tm,),
        in_specs=[pl.BlockSpec((tm, d), lambda i: (i, 0)),
                  pl.BlockSpec((1, d), lambda i: (0, 0)),
                  pl.BlockSpec((None, 1, d), lambda i: (i // tpb, 0, scale_idx)),
                  pl.BlockSpec((None, 1, d), lambda i: (i // tpb, 0, shift_idx))],
        out_specs=pl.BlockSpec((tm, d), lambda i: (i, 0)),
        out_shape=jax.ShapeDtypeStruct((m, d), out_dtype),
        compiler_params=_cp("parallel"),
        name="norm_mod",
    )(x, g.reshape(1, d), mod3, mod3)


def _mm_act_kernel(a_ref, w_ref, o_ref, wbf_ref, *, act):
    @pl.when(pl.program_id(1) == 0)
    def _():
        wbf_ref[...] = w_ref[...].astype(BF16)

    acc = jnp.dot(a_ref[...], wbf_ref[...], preferred_element_type=F32)
    if act == "sigmoid":
        acc = _sigmoid(acc)
    o_ref[...] = acc.astype(o_ref.dtype)


def mm_act(a, w3, layer, n, act, out_dtype, dm, name):
    m, k = a.shape
    tm, tn = dm.tm_dense, dm.tn_wide
    return pl.pallas_call(
        functools.partial(_mm_act_kernel, act=act),
        grid=(n // tn, m // tm),
        in_specs=[pl.BlockSpec((tm, k), lambda j, i: (i, 0)),
                  pl.BlockSpec((None, k, tn), lambda j, i: (layer, 0, j))],
        out_specs=pl.BlockSpec((tm, tn), lambda j, i: (i, j)),
        out_shape=jax.ShapeDtypeStruct((m, n), out_dtype),
        scratch_shapes=[pltpu.VMEM((k, tn), BF16)],
        compiler_params=_cp("arbitrary", "arbitrary"),
        name=name,
    )(a, w3)


def _ifgate_kernel(a_ref, w_ref, wt_ref, bc_ref, br_ref, g_ref, gt_ref, *, hm):
    a = a_ref[...]
    gc = jnp.dot(a, w_ref[...].astype(BF16), preferred_element_type=F32) + bc_ref[...]
    lane = lax.broadcasted_iota(jnp.int32, gc.shape, 1)
    g_ref[...] = jnp.where((lane >= hm) & (lane < 2 * hm), _log_sigmoid(gc), gc)
    gt = lax.dot_general(wt_ref[...].astype(BF16), a, (((1,), (1,)), ((), ())),
                         preferred_element_type=F32) + br_ref[...]
    row = lax.broadcasted_iota(jnp.int32, gt.shape, 0)
    gt_ref[...] = jnp.where((row >= hm) & (row < 2 * hm), _log_sigmoid(gt), gt)


def if_gates(h, w_in, layer, i_bias, f_bias, dm):
    m, k = h.shape
    hm = dm.ml_heads
    tm = dm.tm
    tpb = dm.seq // tm
    c0 = dm.n_main
    wt = jnp.zeros((SUBLANES, k), F32).at[:2 * hm].set(w_in[layer, :, c0:c0 + 2 * hm].T)
    bias = jnp.concatenate([i_bias, f_bias]).astype(F32)
    bc = jnp.zeros((1, LANES), F32).at[0, :2 * hm].set(bias)
    br = jnp.zeros((SUBLANES, 1), F32).at[:2 * hm, 0].set(bias)
    return pl.pallas_call(
        functools.partial(_ifgate_kernel, hm=hm),
        grid=(m // tm,),
        in_specs=[pl.BlockSpec((tm, k), lambda i: (i, 0)),
                  pl.BlockSpec((None, k, LANES), lambda i: (layer, 0, c0 // LANES)),
                  pl.BlockSpec((SUBLANES, k), lambda i: (0, 0)),
                  pl.BlockSpec((1, LANES), lambda i: (0, 0)),
                  pl.BlockSpec((SUBLANES, 1), lambda i: (0, 0))],
        out_specs=[pl.BlockSpec((tm, LANES), lambda i: (i, 0)),
                   pl.BlockSpec((None, SUBLANES, tm), lambda i: (i // tpb, 0, i % tpb))],
        out_shape=[jax.ShapeDtypeStruct((m, LANES), F32),
                   jax.ShapeDtypeStruct((dm.batch, SUBLANES, dm.seq), F32)],
        compiler_params=_cp("parallel"),
        name="if_gates",
    )(h, w_in, wt, bc, br)


def _t5_bucket_np(rel):
    n = jnp.maximum(rel, 0)
    max_exact = REL_BUCKETS // 2
    large = max_exact + (jnp.log(jnp.maximum(n, 1).astype(F32) / max_exact)
                         / math.log(REL_MAX_DIST / max_exact)
                         * (REL_BUCKETS - max_exact)).astype(jnp.int32)
    large = jnp.minimum(large, REL_BUCKETS - 1)
    return jnp.where(n < max_exact, n, large)


def _bias_tile_kernel(tab_ref, idd_ref, ide_ref, d_ref, e_ref):
    h2 = pl.program_id(0)
    far = tab_ref[REL_BUCKETS - 1, h2]
    idd = idd_ref[...]
    ide = ide_ref[...]
    d = jnp.zeros(idd.shape, F32)
    e = jnp.zeros(ide.shape, F32)
    for b in range(REL_BUCKETS):
        t = tab_ref[b, h2] - far
        d = jnp.where(idd == b, t, d)
        e = jnp.where(ide == b, t, e)
    i = lax.broadcasted_iota(jnp.int32, idd.shape, 0)
    j = lax.broadcasted_iota(jnp.int32, idd.shape, 1)
    d_ref[...] = jnp.where(j <= i, d * LOG2E, NEG)
    e_ref[...] = e * LOG2E


def bias_tiles(rel_bias):
    nb, h2 = rel_bias.shape
    i = jnp.arange(LANES, dtype=jnp.int32)[:, None]
    j = jnp.arange(LANES, dtype=jnp.int32)[None, :]
    idd = _t5_bucket_np(i - j)
    ide = _t5_bucket_np(LANES + i - j)
    spec = pl.BlockSpec((LANES, LANES), lambda h: (0, 0))
    ospec = pl.BlockSpec((None, LANES, LANES), lambda h: (h, 0, 0))
    return pl.pallas_call(
        _bias_tile_kernel,
        grid=(h2,),
        in_specs=[pl.BlockSpec(memory_space=pltpu.SMEM), spec, spec],
        out_specs=[ospec, ospec],
        out_shape=[jax.ShapeDtypeStruct((h2, LANES, LANES), F32)] * 2,
        compiler_params=_cp("arbitrary"),
        name="bias_tiles",
    )(rel_bias.astype(F32), idd, ide)


def _attn_kernel(q_ref, k_ref, v_ref, d0_ref, e0_ref, lam_ref, subg_ref, gq_ref, gk_ref, o_ref,
                 qq_ref, kn_ref, m_ref, acc_ref, bd_ref, be_ref, *, tq, rc, lam_init):
    qi = pl.program_id(2)
    nsub = tq // LANES

    @pl.when(qi == 0)
    def _():
        for mp in range(2):
            for a in range(nsub):
                for b in range(nsub):
                    rows = pl.ds(mp * tq + a * LANES, LANES)
                    cols = pl.ds(b * LANES, LANES)
                    if a == b:
                        bd_ref[rows, cols] = d0_ref[mp]
                    elif a == b + 1:
                        bd_ref[rows, cols] = e0_ref[mp]
                    elif a < b:
                        bd_ref[rows, cols] = jnp.full((LANES, LANES), NEG, F32)
                    else:
                        bd_ref[rows, cols] = jnp.zeros((LANES, LANES), F32)
                    if a == 0 and b == nsub - 1:
                        be_ref[rows, cols] = e0_ref[mp]
                    else:
                        be_ref[rows, cols] = jnp.zeros((LANES, LANES), F32)

        def norm_keys(ci, carry):
            rows = pl.ds(pl.multiple_of(ci * tq, tq), tq)
            kf = k_ref[rows, :].astype(F32)
            low = lax.broadcasted_iota(jnp.int32, kf.shape, 1) < DA_DQK
            sq = kf * kf
            ss1 = jnp.sum(jnp.where(low, sq, 0.0), axis=1, keepdims=True)
            ss2 = jnp.sum(jnp.where(low, 0.0, sq), axis=1, keepdims=True)
            rs = lax.rsqrt(jnp.where(low, ss1, ss2) * (1.0 / DA_DQK) + EPS)
            kn_ref[rows, :] = (kf * rs * gk_ref[...]).astype(BF16)
            return carry

        lax.fori_loop(0, k_ref.shape[0] // tq, norm_keys, 0)

    qf = q_ref[...].astype(F32)
    lane = lax.broadcasted_iota(jnp.int32, qf.shape, 1)
    for mp in range(2):
        qm = jnp.where((lane < DA_DQK) == (mp == 0), qf, 0.0)
        ss = jnp.sum(qm * qm, axis=1, keepdims=True)
        qn = qm * lax.rsqrt(ss * (1.0 / DA_DQK) + EPS) * gq_ref[...]
        qq_ref[pl.ds(mp * tq, tq), :] = qn.astype(BF16)
    m_ref[...] = jnp.full(m_ref.shape, NEG, F32)
    acc_ref[...] = jnp.zeros(acc_ref.shape, F32)

    def step(start, bias_ref):
        k = kn_ref[pl.ds(start, tq), :]
        v = v_ref[pl.ds(start, tq), :]
        va = jnp.concatenate([v, jnp.ones_like(v)], axis=1)
        for r in range(2 * tq // rc):
            rows = pl.ds(r * rc, rc)
            s = lax.dot_general(qq_ref[rows, :], k, (((1,), (1,)), ((), ())),
                                preferred_element_type=F32)
            if bias_ref is not None:
                s = s + bias_ref[rows, :]
            m_prev = m_ref[rows, :]
            m_new = jnp.maximum(m_prev, jnp.max(s, axis=1, keepdims=True))
            alpha = jnp.exp2(m_prev - m_new)
            p = jnp.exp2(s - jnp.concatenate([m_new] * (tq // LANES), axis=1))
            pv = jnp.dot(p.astype(BF16), va, preferred_element_type=F32)
            acc_ref[rows, :] = jnp.concatenate([alpha, alpha], axis=1) * acc_ref[rows, :] + pv
            m_ref[rows, :] = m_new

    def far_body(ki, carry):
        step(pl.multiple_of(ki * tq, tq), None)
        return carry

    lax.fori_loop(0, jnp.maximum(qi - 1, 0), far_body, 0)

    @pl.when(qi >= 1)
    def _():
        step(pl.multiple_of((qi - 1) * tq, tq), be_ref)

    step(pl.multiple_of(qi * tq, tq), bd_ref)

    acc = acc_ref[...]
    o = acc[:, :DA_DV] / acc[:, DA_DV:]
    lamv = lam_ref[...]
    lam = (jnp.exp(jnp.sum(lamv[0:1] * lamv[1:2], axis=1, keepdims=True))
           - jnp.exp(jnp.sum(lamv[2:3] * lamv[3:4], axis=1, keepdims=True)) + lam_init)
    od = o[:tq] - lam * o[tq:]
    ms = jnp.mean(od * od, axis=-1, keepdims=True)
    y = od * lax.rsqrt(ms + EPS) * subg_ref[...] * (1.0 - lam_init)
    o_ref[...] = y.astype(o_ref.dtype)


def diff_attention(proj, d0, e0, lamv, sub_g, q_g, k_g, lam_init, dm):
    m = proj.shape[0]
    t, tq, h = dm.seq, dm.tq, dm.da_heads
    nq = t // tq
    gq = (jnp.tile(q_g.astype(F32), 2) * (DA_DQK ** -0.5 * LOG2E)).reshape(1, LANES)
    gk = jnp.tile(k_g.astype(F32), 2).reshape(1, LANES)
    vec = pl.BlockSpec((1, LANES), lambda b, hh, qi: (0, 0))
    return pl.pallas_call(
        functools.partial(_attn_kernel, tq=tq, rc=min(dm.attn_rc, 2 * tq), lam_init=lam_init),
        grid=(dm.batch, h, nq),
        in_specs=[pl.BlockSpec((tq, LANES), lambda b, hh, qi: (b * nq + qi, hh)),
                  pl.BlockSpec((t, LANES), lambda b, hh, qi: (b, h + hh)),
                  pl.BlockSpec((t, LANES), lambda b, hh, qi: (b, 2 * h + hh)),
                  pl.BlockSpec((2, LANES, LANES), lambda b, hh, qi: (hh, 0, 0)),
                  pl.BlockSpec((2, LANES, LANES), lambda b, hh, qi: (hh, 0, 0)),
                  pl.BlockSpec((4, DA_DQK), lambda b, hh, qi: (0, 0)),
                  vec, vec, vec],
        out_specs=pl.BlockSpec((tq, LANES), lambda b, hh, qi: (b * nq + qi, hh)),
        out_shape=jax.ShapeDtypeStruct((m, dm.d_branch), BF16),
        scratch_shapes=[pltpu.VMEM((2 * tq, LANES), BF16),
                        pltpu.VMEM((t, LANES), BF16),
                        pltpu.VMEM((2 * tq, LANES), F32),
                        pltpu.VMEM((2 * tq, 2 * LANES), F32),
                        pltpu.VMEM((2 * tq, tq), F32),
                        pltpu.VMEM((2 * tq, tq), F32)],
        compiler_params=_cp("parallel", "parallel", "arbitrary"),
        name="diff_attention",
    )(proj, proj, proj, d0, e0, lamv, sub_g.reshape(1, DA_DV), gq, gk)


def _mlstm_kernel(q_ref, k_ref, v_ref, og_ref, g_ref, gt_ref, cwq_ref, cwk_ref,
                  cbq_ref, cbk_ref, ng_ref, o_ref,
                  qbuf_ref, kbuf_ref, ct_ref, n_ref, m_ref, *, L, hm):
    h = pl.program_id(1)
    c = pl.program_id(2)

    @pl.when(c == 0)
    def _():
        qbuf_ref[pl.ds(0, SUBLANES), :] = jnp.zeros((SUBLANES, ML_DQK), F32)
        kbuf_ref[pl.ds(0, SUBLANES), :] = jnp.zeros((SUBLANES, ML_DQK), F32)
        ct_ref[...] = jnp.zeros(ct_ref.shape, F32)
        n_ref[...] = jnp.zeros(n_ref.shape, F32)
        m_ref[...] = jnp.zeros(m_ref.shape, F32)

    def conv_silu(x_ref, buf_ref, w_ref, b_ref):
        buf_ref[pl.ds(SUBLANES, L), :] = x_ref[...].astype(F32)
        w = w_ref[...]
        y = b_ref[...] + w[CONV_W - 1:CONV_W] * buf_ref[pl.ds(SUBLANES, L), :]
        for s in range(1, CONV_W):
            y = y + w[CONV_W - 1 - s:CONV_W - s] * buf_ref[pl.ds(SUBLANES - s, L), :]
        tail = buf_ref[pl.ds(L, SUBLANES), :]
        buf_ref[pl.ds(0, SUBLANES), :] = tail
        return y * _sigmoid(y)

    q = conv_silu(q_ref, qbuf_ref, cwq_ref, cbq_ref)
    k = conv_silu(k_ref, kbuf_ref, cwk_ref, cbk_ref) * (ML_DQK ** -0.5)
    v = v_ref[...]

    g = g_ref[...]
    gt = gt_ref[...]
    ii = lax.broadcasted_iota(jnp.int32, (L, L), 0)
    jj = lax.broadcasted_iota(jnp.int32, (L, L), 1)
    tril = (jj <= ii).astype(F32)
    triu = (ii <= jj).astype(F32)
    bcum = jnp.dot(tril, g, preferred_element_type=F32, precision=lax.Precision.HIGHEST)
    btcum = jnp.dot(gt, triu, preferred_element_type=F32, precision=lax.Precision.HIGHEST)
    lane = lax.broadcasted_iota(jnp.int32, g.shape, 1)
    ig_col = jnp.sum(jnp.where(lane == h, g, 0.0), axis=1, keepdims=True)
    b_col = jnp.sum(jnp.where(lane == hm + h, bcum, 0.0), axis=1, keepdims=True)
    row = lax.broadcasted_iota(jnp.int32, gt.shape, 0)
    ig_row = jnp.sum(jnp.where(row == h, gt, 0.0), axis=0, keepdims=True)
    b_row = jnp.sum(jnp.where(row == hm + h, btcum, 0.0), axis=0, keepdims=True)

    m_old = m_ref[...]
    dlog = jnp.where(jj <= ii, b_col - b_row + ig_row, NEG)
    m_inter = b_col + m_old
    m_s = jnp.maximum(m_inter, jnp.max(dlog, axis=1, keepdims=True))
    dmat = jnp.exp(dlog - m_s)
    qb = q.astype(BF16)
    kb = k.astype(BF16)
    qk = lax.dot_general(qb, kb, (((1,), (1,)), ((), ())), preferred_element_type=F32)
    w = dmat * qk
    inter = jnp.exp(m_inter - m_s)
    ct = ct_ref[...]
    n_row = n_ref[...]
    num = (jnp.dot(w.astype(BF16), v, preferred_element_type=F32)
           + inter * jnp.dot(qb, ct.astype(BF16), preferred_element_type=F32))
    den = (jnp.sum(w, axis=1, keepdims=True)
           + inter * jnp.sum(q * n_row, axis=1, keepdims=True))
    hh = num / jnp.maximum(jnp.abs(den), jnp.exp(-m_s))

    m_new = m_s[L - 1:L]
    b_last = b_col[L - 1:L]
    w_end = jnp.exp(b_last - b_col + ig_col - m_new)
    decay = jnp.exp(b_last + m_old - m_new)
    kw = k * w_end
    ct_ref[...] = decay * ct + lax.dot_general(kw.astype(BF16), v, (((0,), (0,)), ((), ())),
                                               preferred_element_type=F32)
    n_ref[...] = decay * n_row + jnp.sum(kw, axis=0, keepdims=True)
    m_ref[...] = m_new

    ms = jnp.mean(hh * hh, axis=-1, keepdims=True)
    y = hh * lax.rsqrt(ms + EPS) * ng_ref[...] * _sigmoid(og_ref[...].astype(F32))
    o_ref[...] = y.astype(o_ref.dtype)


def mlstm_branch(proj, g, gt, conv_w, conv_b, norm_g, dm):
    m = proj.shape[0]
    t, L, hm, h = dm.seq, dm.ml_chunk, dm.ml_heads, dm.da_heads
    nc = t // L
    qc0 = 3 * h
    kc0 = 3 * h + hm
    vc0 = 4 * hm
    oc0 = 5 * hm
    cw = conv_w.astype(F32)
    cb = conv_b.astype(F32).reshape(1, -1)
    return pl.pallas_call(
        functools.partial(_mlstm_kernel, L=L, hm=hm),
        grid=(dm.batch, hm, nc),
        in_specs=[pl.BlockSpec((L, ML_DQK), lambda b, hh, c: (b * nc + c, qc0 + hh)),
                  pl.BlockSpec((L, ML_DQK), lambda b, hh, c: (b * nc + c, kc0 + hh)),
                  pl.BlockSpec((L, ML_DV), lambda b, hh, c: (b * nc + c, vc0 + hh)),
                  pl.BlockSpec((L, ML_DV), lambda b, hh, c: (b * nc + c, oc0 + hh)),
                  pl.BlockSpec((L, LANES), lambda b, hh, c: (b * nc + c, 0)),
                  pl.BlockSpec((None, SUBLANES, L), lambda b, hh, c: (b, 0, c)),
                  pl.BlockSpec((CONV_W, ML_DQK), lambda b, hh, c: (0, hh)),
                  pl.BlockSpec((CONV_W, ML_DQK), lambda b, hh, c: (0, hm + hh)),
                  pl.BlockSpec((1, ML_DQK), lambda b, hh, c: (0, hh)),
                  pl.BlockSpec((1, ML_DQK), lambda b, hh, c: (0, hm + hh)),
                  pl.BlockSpec((1, ML_DV), lambda b, hh, c: (0, hh))],
        out_specs=pl.BlockSpec((L, ML_DV), lambda b, hh, c: (b * nc + c, hh)),
        out_shape=jax.ShapeDtypeStruct((m, dm.d_branch), BF16),
        scratch_shapes=[pltpu.VMEM((L + SUBLANES, ML_DQK), F32),
                        pltpu.VMEM((L + SUBLANES, ML_DQK), F32),
                        pltpu.VMEM((ML_DQK, ML_DV), F32),
                        pltpu.VMEM((1, ML_DQK), F32),
                        pltpu.VMEM((1, 1), F32)],
        compiler_params=_cp("parallel", "parallel", "arbitrary"),
        name="mlstm_branch",
    )(proj, proj, proj, proj, g, gt, cw, cw, cb, cb, norm_g.reshape(1, -1))


def _merge_kernel(ya_ref, yb_ref, wa_ref, wb_ref, sa_ref, sb_ref, o_ref, wabf_ref, wbbf_ref):
    @pl.when(pl.program_id(1) == 0)
    def _():
        wabf_ref[...] = wa_ref[...].astype(BF16)
        wbbf_ref[...] = wb_ref[...].astype(BF16)

    pa = jnp.dot(ya_ref[...], wabf_ref[...], preferred_element_type=F32)
    pb = jnp.dot(yb_ref[...], wbbf_ref[...], preferred_element_type=F32)
    o_ref[...] = (sa_ref[...].astype(F32) * pa + sb_ref[...].astype(F32) * pb).astype(o_ref.dtype)


def merge_branches(ya, yb, wa, wb, layer, sg, dm):
    m, kb = ya.shape
    d = dm.d_model
    tm, tn = dm.tm_dense, dm.tn
    nj = d // tn
    return pl.pallas_call(
        _merge_kernel,
        grid=(nj, m // tm),
        in_specs=[pl.BlockSpec((tm, kb), lambda j, i: (i, 0)),
                  pl.BlockSpec((tm, kb), lambda j, i: (i, 0)),
                  pl.BlockSpec((None, kb, tn), lambda j, i: (layer, 0, j)),
                  pl.BlockSpec((None, kb, tn), lambda j, i: (layer, 0, j)),
                  pl.BlockSpec((tm, tn), lambda j, i: (i, j)),
                  pl.BlockSpec((tm, tn), lambda j, i: (i, nj + j))],
        out_specs=pl.BlockSpec((tm, tn), lambda j, i: (i, j)),
        out_shape=jax.ShapeDtypeStruct((m, d), BF16),
        scratch_shapes=[pltpu.VMEM((kb, tn), BF16), pltpu.VMEM((kb, tn), BF16)],
        compiler_params=_cp("arbitrary", "arbitrary"),
        name="merge_branches",
    )(ya, yb, wa, wb, sg, sg)


def _mm_res_kernel(a_ref, w_ref, x_ref, gate_ref, o_ref, wbf_ref):
    @pl.when(pl.program_id(1) == 0)
    def _():
        wbf_ref[...] = w_ref[...].astype(BF16)

    acc = jnp.dot(a_ref[...], wbf_ref[...], preferred_element_type=F32)
    o_ref[...] = x_ref[...] + gate_ref[...] * acc


def mm_residual(a, w3, layer, x, mod3, gate_idx, dm):
    m, k = a.shape
    d = x.shape[1]
    wide = k <= dm.d_model
    tm = dm.tm_dense if wide else dm.tm
    tn = dm.tn_wide if wide else dm.tn
    nj = d // tn
    tpb = dm.seq // tm
    return pl.pallas_call(
        _mm_res_kernel,
        grid=(nj, m // tm),
        in_specs=[pl.BlockSpec((tm, k), lambda j, i: (i, 0)),
                  pl.BlockSpec((None, k, tn), lambda j, i: (layer, 0, j)),
                  pl.BlockSpec((tm, tn), lambda j, i: (i, j)),
                  pl.BlockSpec((None, 1, tn), lambda j, i: (i // tpb, 0, gate_idx * nj + j))],
        out_specs=pl.BlockSpec((tm, tn), lambda j, i: (i, j)),
        out_shape=jax.ShapeDtypeStruct((m, d), F32),
        scratch_shapes=[pltpu.VMEM((k, tn), BF16)],
        compiler_params=_cp("arbitrary", "arbitrary"),
        name="mm_residual",
    )(a, w3, x, mod3)


def _swiglu(a, w1, w3):
    p1 = jnp.dot(a, w1, preferred_element_type=F32)
    p3 = jnp.dot(a, w3, preferred_element_type=F32)
    return (p1 * _sigmoid(p1) * p3).astype(BF16)


def _up_kernel(a_ref, w1_ref, w3_ref, o_ref, w1bf_ref, w3bf_ref):
    @pl.when(pl.program_id(1) == 0)
    def _():
        w1bf_ref[...] = w1_ref[...].astype(BF16)
        w3bf_ref[...] = w3_ref[...].astype(BF16)

    o_ref[...] = _swiglu(a_ref[...], w1bf_ref[...], w3bf_ref[...])


def ffn_up(a, w1, w3, layer, dm):
    m, k = a.shape
    f = w1.shape[-1]
    tm, tn = dm.tm_dense, dm.tn
    return pl.pallas_call(
        _up_kernel,
        grid=(f // tn, m // tm),
        in_specs=[pl.BlockSpec((tm, k), lambda j, i: (i, 0)),
                  pl.BlockSpec((None, k, tn), lambda j, i: (layer, 0, j)),
                  pl.BlockSpec((None, k, tn), lambda j, i: (layer, 0, j))],
        out_specs=pl.BlockSpec((tm, tn), lambda j, i: (i, j)),
        out_shape=jax.ShapeDtypeStruct((m, f), BF16),
        scratch_shapes=[pltpu.VMEM((k, tn), BF16), pltpu.VMEM((k, tn), BF16)],
        compiler_params=_cp("arbitrary", "arbitrary"),
        name="ffn_up",
    )(a, w1, w3)


def _gup_kernel(te_ref, tv_ref, a_ref, w1_ref, w3_ref, o_ref, w1bf_ref, w3bf_ref):
    i = pl.program_id(1)
    changed = (i == 0) | (te_ref[i] != te_ref[jnp.maximum(i - 1, 0)])

    @pl.when(changed)
    def _():
        w1bf_ref[...] = w1_ref[...].astype(BF16)
        w3bf_ref[...] = w3_ref[...].astype(BF16)

    @pl.when(tv_ref[i] == 1)
    def _():
        o_ref[...] = _swiglu(a_ref[...], w1bf_ref[...], w3bf_ref[...])

    @pl.when(tv_ref[i] == 0)
    def _():
        o_ref[...] = jnp.zeros(o_ref.shape, o_ref.dtype)


def moe_up(hs, w1, w3, lm, te, tv, dm):
    p, k = hs.shape
    f = w1.shape[-1]
    tg, tn = dm.tg, dm.tn
    grid_spec = pltpu.PrefetchScalarGridSpec(
        num_scalar_prefetch=2,
        grid=(f // tn, p // tg),
        in_specs=[pl.BlockSpec((tg, k), lambda j, i, te, tv: (i, 0)),
                  pl.BlockSpec((None, None, k, tn), lambda j, i, te, tv: (lm, te[i], 0, j)),
                  pl.BlockSpec((None, None, k, tn), lambda j, i, te, tv: (lm, te[i], 0, j))],
        out_specs=pl.BlockSpec((tg, tn), lambda j, i, te, tv: (i, j)),
        scratch_shapes=[pltpu.VMEM((k, tn), BF16), pltpu.VMEM((k, tn), BF16)])
    return pl.pallas_call(
        _gup_kernel,
        grid_spec=grid_spec,
        out_shape=jax.ShapeDtypeStruct((p, f), BF16),
        compiler_params=_cp("arbitrary", "arbitrary"),
        name="moe_up",
    )(te, tv, hs, w1, w3)


def _gdown_kernel(te_ref, tv_ref, u_ref, w_ref, o_ref, wbf_ref):
    i = pl.program_id(1)
    changed = (i == 0) | (te_ref[i] != te_ref[jnp.maximum(i - 1, 0)])

    @pl.when(changed)
    def _():
        wbf_ref[...] = w_ref[...].astype(BF16)

    @pl.when(tv_ref[i] == 1)
    def _():
        o_ref[...] = jnp.dot(u_ref[...], wbf_ref[...], preferred_element_type=F32)

    @pl.when(tv_ref[i] == 0)
    def _():
        o_ref[...] = jnp.zeros(o_ref.shape, o_ref.dtype)


def moe_down(u, w2, lm, te, tv, dm):
    p, f = u.shape
    d = w2.shape[-1]
    tg, tn = dm.tg, dm.tn
    grid_spec = pltpu.PrefetchScalarGridSpec(
        num_scalar_prefetch=2,
        grid=(d // tn, p // tg),
        in_specs=[pl.BlockSpec((tg, f), lambda j, i, te, tv: (i, 0)),
                  pl.BlockSpec((None, None, f, tn), lambda j, i, te, tv: (lm, te[i], 0, j))],
        out_specs=pl.BlockSpec((tg, tn), lambda j, i, te, tv: (i, j)),
        scratch_shapes=[pltpu.VMEM((f, tn), BF16)])
    return pl.pallas_call(
        _gdown_kernel,
        grid_spec=grid_spec,
        out_shape=jax.ShapeDtypeStruct((p, d), F32),
        compiler_params=_cp("arbitrary", "arbitrary"),
        name="moe_down",
    )(te, tv, u, w2)


def _router_kernel(a_ref, w_ref, o_ref, *, n_experts):
    logits = jnp.dot(a_ref[...].astype(BF16), w_ref[...].astype(BF16),
                     preferred_element_type=F32)
    lane = lax.broadcasted_iota(jnp.int32, logits.shape, 1)
    lg = jnp.where(lane < n_experts, logits, NEG)
    v1 = jnp.max(lg, axis=1, keepdims=True)
    i1 = jnp.min(jnp.where(lg == v1, lane, LANES), axis=1, keepdims=True)
    lg2 = jnp.where(lane == i1, NEG, lg)
    v2 = jnp.max(lg2, axis=1, keepdims=True)
    i2 = jnp.min(jnp.where(lg2 == v2, lane, LANES), axis=1, keepdims=True)
    e2 = jnp.exp(v2 - v1)
    g1 = 1.0 / (1.0 + e2)
    g2 = e2 / (1.0 + e2)
    out = jnp.where(lane == 0, g1, 0.0)
    out = jnp.where(lane == 1, g2, out)
    out = jnp.where(lane == 2, i1.astype(F32), out)
    out = jnp.where(lane == 3, i2.astype(F32), out)
    o_ref[...] = out


def router(hf, router_w, lm, dm):
    m, k = hf.shape
    e = dm.n_experts
    tm = dm.tm
    wpad = jnp.zeros((k, LANES), F32).at[:, :e].set(router_w[lm])
    return pl.pallas_call(
        functools.partial(_router_kernel, n_experts=e),
        grid=(m // tm,),
        in_specs=[pl.BlockSpec((tm, k), lambda i: (i, 0)),
                  pl.BlockSpec((k, LANES), lambda i: (0, 0))],
        out_specs=pl.BlockSpec((tm, LANES), lambda i: (i, 0)),
        out_shape=jax.ShapeDtypeStruct((m, LANES), F32),
        compiler_params=_cp("parallel"),
        name="router",
    )(hf, wpad)


def _row_copy(src_ref, dst_ref, src_row, dst_row, sem):
    return pltpu.make_async_copy(src_ref.at[pl.ds(src_row, 1), :],
                                 dst_ref.at[pl.ds(dst_row, 1), :], sem)


def _gather_kernel(idx_ref, src_ref, o_ref, buf_ref, sem, *, rows):
    def issue(r, carry):
        _row_copy(src_ref, buf_ref, idx_ref[0, 0, r], r, sem).start()
        return carry

    lax.fori_loop(0, rows, issue, 0, unroll=DMA_UNROLL)

    def wait(r, carry):
        _row_copy(src_ref, buf_ref, 0, r, sem).wait()
        return carry

    lax.fori_loop(0, rows, wait, 0, unroll=DMA_UNROLL)
    o_ref[...] = buf_ref[...].astype(o_ref.dtype)


def gather_rows(src, idx, rows, out_dtype):
    p = idx.shape[0]
    d = src.shape[1]
    nblk = p // rows
    return pl.pallas_call(
        functools.partial(_gather_kernel, rows=rows),
        grid=(nblk,),
        in_specs=[pl.BlockSpec((1, 1, rows), lambda i: (i, 0, 0), memory_space=pltpu.SMEM),
                  pl.BlockSpec(memory_space=pl.ANY)],
        out_specs=pl.BlockSpec((rows, d), lambda i: (i, 0)),
        out_shape=jax.ShapeDtypeStruct((p, d), out_dtype),
        scratch_shapes=[pltpu.VMEM((rows, d), src.dtype), pltpu.SemaphoreType.DMA(())],
        compiler_params=_cp("arbitrary"),
        name="gather_rows",
    )(idx.reshape(nblk, 1, rows), src)


def _combine_kernel(pos_ref, ys_ref, x_ref, r_ref, gate_ref, o_ref, buf_ref, sem, *, rows):
    def issue(r, carry):
        _row_copy(ys_ref, buf_ref.at[0], pos_ref[0, 0, 2 * r], r, sem).start()
        _row_copy(ys_ref, buf_ref.at[1], pos_ref[0, 0, 2 * r + 1], r, sem).start()
        return carry

    lax.fori_loop(0, rows, issue, 0, unroll=DMA_UNROLL)

    def wait(r, carry):
        _row_copy(ys_ref, buf_ref.at[0], 0, r, sem).wait()
        _row_copy(ys_ref, buf_ref.at[1], 0, r, sem).wait()
        return carry

    lax.fori_loop(0, rows, wait, 0, unroll=DMA_UNROLL)
    rr = r_ref[...]
    mix = rr[:, 0:1] * buf_ref[0] + rr[:, 1:2] * buf_ref[1]
    o_ref[...] = x_ref[...] + gate_ref[...] * mix


def moe_combine(ys, pos, x, rinfo, mod3, gate_idx, dm):
    m, d = x.shape
    rows = dm.tc
    nblk = m // rows
    tpb = dm.seq // rows
    return pl.pallas_call(
        functools.partial(_combine_kernel, rows=rows),
        grid=(nblk,),
        in_specs=[pl.BlockSpec((1, 1, 2 * rows), lambda i: (i, 0, 0), memory_space=pltpu.SMEM),
                  pl.BlockSpec(memory_space=pl.ANY),
                  pl.BlockSpec((rows, d), lambda i: (i, 0)),
                  pl.BlockSpec((rows, LANES), lambda i: (i, 0)),
                  pl.BlockSpec((None, 1, d), lambda i: (i // tpb, 0, gate_idx))],
        out_specs=pl.BlockSpec((rows, d), lambda i: (i, 0)),
        out_shape=jax.ShapeDtypeStruct((m, d), F32),
        scratch_shapes=[pltpu.VMEM((2, rows, d), F32), pltpu.SemaphoreType.DMA(())],
        compiler_params=_cp("arbitrary"),
        name="moe_combine",
    )(pos.reshape(nblk, 1, 2 * rows), ys, x, rinfo, mod3)


def routing_tables(rinfo, dm):
    m = rinfo.shape[0]
    e, tg = dm.n_experts, dm.tg
    n_tiles = (TOP_K * m) // tg + e
    eid = rinfo[:, 2:2 + TOP_K].astype(jnp.int32).reshape(-1)
    onehot = (eid[:, None] == jnp.arange(e, dtype=jnp.int32)[None, :]).astype(jnp.int32)
    csum = jnp.cumsum(onehot, axis=0)
    rank = jnp.sum((csum - onehot) * onehot, axis=1)
    counts = csum[-1]
    tiles_per = (counts + tg - 1) // tg
    tile_end = jnp.cumsum(tiles_per)
    tile_start = tile_end - tiles_per
    pos = tile_start[eid] * tg + rank
    token = jnp.arange(TOP_K * m, dtype=jnp.int32) // TOP_K
    slot_token = jnp.zeros((n_tiles * tg,), jnp.int32).at[pos].set(token)
    tidx = jnp.arange(n_tiles, dtype=jnp.int32)
    used = tile_end[-1]
    te_raw = jnp.sum((tidx[:, None] >= tile_end[None, :]).astype(jnp.int32), axis=1)
    last_e = jnp.sum((jnp.maximum(used - 1, 0) >= tile_end).astype(jnp.int32))
    te = jnp.where(tidx < used, te_raw, last_e).astype(jnp.int32)
    tv = (tidx < used).astype(jnp.int32)
    return pos.astype(jnp.int32), slot_token, te, tv


def forward(dm, x, c, rel_bias, w_mod, b_mod, norm_mix_g, norm_ff_g, w_in,
            da_q_g, da_k_g, da_lam_q1, da_lam_k1, da_lam_q2, da_lam_k2, da_sub_g,
            ml_conv_w, ml_conv_b, ml_i_bias, ml_f_bias, ml_norm_g,
            w_branch_a, w_branch_b, w_out,
            ffn_w1, ffn_w3, ffn_w2, router_w, moe_w1, moe_w3, moe_w2):
    b, t, d = x.shape
    m = b * t
    h = dm.da_heads
    xf = x.astype(F32).reshape(m, d)
    mod_all = adaln_mod(c, w_mod, b_mod)
    d0, e0 = bias_tiles(rel_bias)
    c_gate = dm.n_main + 2 * dm.ml_heads
    w_g = w_in[:, :, c_gate:]

    for l in range(dm.depth):
        mod3 = mod_all[l].reshape(b, 1, 6 * d)
        hb = norm_mod(xf, norm_mix_g[l], mod3, 1, 0, dm, BF16)
        proj = mm_act(hb, w_in, l, dm.n_main, None, BF16, dm, "proj_main")
        sg = mm_act(hb, w_g, l, 2 * d, "sigmoid", BF16, dm, "branch_gates")
        g, gt = if_gates(hb, w_in, l, ml_i_bias[l], ml_f_bias[l], dm)
        lam_init = 0.8 - 0.6 * math.exp(-0.3 * l)
        lamv = jnp.stack([da_lam_q1[l], da_lam_k1[l], da_lam_q2[l], da_lam_k2[l]]).astype(F32)
        ya = diff_attention(proj, d0, e0, lamv, da_sub_g[l].astype(F32), da_q_g[l], da_k_g[l],
                            lam_init, dm)
        yb = mlstm_branch(proj, g, gt, ml_conv_w[l], ml_conv_b[l], ml_norm_g[l].astype(F32), dm)
        merged = merge_branches(ya, yb, w_branch_a, w_branch_b, l, sg, dm)
        xf = mm_residual(merged, w_out, l, xf, mod3, 2, dm)
        if l % 2 == 0:
            hb = norm_mod(xf, norm_ff_g[l], mod3, 4, 3, dm, BF16)
            u = ffn_up(hb, ffn_w1, ffn_w3, l // 2, dm)
            xf = mm_residual(u, ffn_w2, l // 2, xf, mod3, 5, dm)
        else:
            lm = l // 2
            hf = norm_mod(xf, norm_ff_g[l], mod3, 4, 3, dm, F32)
            rinfo = router(hf, router_w, lm, dm)
            pos, slot_token, te, tv = routing_tables(rinfo, dm)
            hs = gather_rows(hf, slot_token, dm.tg, BF16)
            u = moe_up(hs, moe_w1, moe_w3, lm, te, tv, dm)
            ys = moe_down(u, moe_w2, lm, te, tv, dm)
            xf = moe_combine(ys, pos, xf, rinfo, mod3, 5, dm)
    return xf.reshape(b, t, d)


def kernel(x, c, rel_bias, w_mod, b_mod, norm_mix_g, norm_ff_g, w_in, da_q_g, da_k_g, da_lam_q1, da_lam_k1, da_lam_q2, da_lam_k2, da_sub_g, ml_conv_w, ml_conv_b, ml_i_bias, ml_f_bias, ml_norm_g, w_branch_a, w_branch_b, w_out, ffn_w1, ffn_w3, ffn_w2, router_w, moe_w1, moe_w3, moe_w2):
    b, t, d = x.shape
    dm = Dims(d_model=d, batch=b, seq=t, depth=w_mod.shape[0], d_ff=ffn_w1.shape[-1],
              n_experts=router_w.shape[-1])
    return forward(dm, x, c, rel_bias, w_mod, b_mod, norm_mix_g, norm_ff_g, w_in,
                   da_q_g, da_k_g, da_lam_q1, da_lam_k1, da_lam_q2, da_lam_k2, da_sub_g,
                   ml_conv_w, ml_conv_b, ml_i_bias, ml_f_bias, ml_norm_g,
                   w_branch_a, w_branch_b, w_out,
                   ffn_w1, ffn_w3, ffn_w2, router_w, moe_w1, moe_w3, moe_w2)
```

```python
import functools
import math
from typing import NamedTuple

import numpy as np
import jax
import jax.numpy as jnp
from jax import lax
from jax.experimental import pallas as pl
from jax.experimental.pallas import tpu as pltpu

F32 = jnp.float32
BF16 = jnp.bfloat16

DA_DQK = 64
DA_DV = 128
ML_DQK = 128
ML_DV = 256
CONV_W = 4
TOP_K = 2
REL_BUCKETS = 32
REL_MAX_DIST = 128
EPS = 1e-6
NEG = -1e30
LOG2E = math.log2(math.e)
LANES = 128
SUBLANES = 8
VMEM_LIMIT = 52 * 1024 * 1024
DMA_UNROLL = 8


class Dims(NamedTuple):
    d_model: int
    batch: int
    seq: int
    depth: int
    d_ff: int
    n_experts: int
    tm: int = 512
    tm_dense: int = 1024
    tn: int = 512
    tn_wide: int = 1024
    tq: int = 512
    ml_chunk: int = 256
    tg: int = 512
    tc: int = 256
    attn_rc: int = 512

    @property
    def d_branch(self):
        return self.d_model // 2

    @property
    def da_heads(self):
        return self.d_branch // DA_DV

    @property
    def ml_heads(self):
        return self.d_branch // ML_DV

    @property
    def n_main(self):
        return 3 * self.d_branch + 2 * self.ml_heads * ML_DQK + 2 * self.d_branch

    @property
    def tokens(self):
        return self.batch * self.seq


def _cp(*sem):
    return pltpu.CompilerParams(dimension_semantics=sem, vmem_limit_bytes=VMEM_LIMIT)


def _sigmoid(x):
    return 1.0 / (1.0 + jnp.exp(-x))


def _log_sigmoid(x):
    return jnp.minimum(x, 0.0) - jnp.log(1.0 + jnp.exp(-jnp.abs(x)))


def _mod_kernel(c_ref, w_ref, b_ref, o_ref):
    c = c_ref[...]
    ca = c * _sigmoid(c)
    o_ref[...] = jnp.dot(ca, w_ref[...], preferred_element_type=F32,
                         precision=lax.Precision.HIGHEST) + b_ref[...]


def adaln_mod(c, w_mod, b_mod):
    depth, d, n = w_mod.shape
    b = c.shape[0]
    bp = -(-b // SUBLANES) * SUBLANES
    cp = jnp.zeros((bp, d), F32).at[:b].set(c.astype(F32))
    tn = min(n, 1024)
    out = pl.pallas_call(
        _mod_kernel,
        grid=(depth, n // tn),
        in_specs=[pl.BlockSpec((bp, d), lambda l, j: (0, 0)),
                  pl.BlockSpec((None, d, tn), lambda l, j: (l, 0, j)),
                  pl.BlockSpec((None, 1, tn), lambda l, j: (l, 0, j))],
        out_specs=pl.BlockSpec((None, bp, tn), lambda l, j: (l, 0, j)),
        out_shape=jax.ShapeDtypeStruct((depth, bp, n), F32),
        compiler_params=_cp("parallel", "parallel"),
        name="adaln_mod",
    )(cp, w_mod, b_mod.reshape(depth, 1, n))
    return out[:, :b]


def _norm_mod_kernel(x_ref, g_ref, sc_ref, sh_ref, o_ref):
    x = x_ref[...]
    ms = jnp.mean(x * x, axis=-1, keepdims=True)
    h = x * lax.rsqrt(ms + EPS) * g_ref[...] * (1.0 + sc_ref[...]) + sh_ref[...]
    o_ref[...] = h.astype(o_ref.dtype)


def norm_mod(x, g, mod3, scale_idx, shift_idx, dm, out_dtype):
    m, d = x.shape
    tm = dm.tm
    tpb = dm.seq // tm
    return pl.pallas_call(
        _norm_mod_kernel,
        grid=(m // tm,),
        in_specs=[pl.BlockSpec((tm, d), lambda i: (i, 0)),
                  pl.BlockSpec((1, d), lambda i: (0, 0)),
                  pl.BlockSpec((None, 1, d), lambda i: (i // tpb, 0, scale_idx)),
                  pl.BlockSpec((None, 1, d), lambda i: (i // tpb, 0, shift_idx))],
        out_specs=pl.BlockSpec((tm, d), lambda i: (i, 0)),
        out_shape=jax.ShapeDtypeStruct((m, d), out_dtype),
        compiler_params=_cp("parallel"),
        name="norm_mod",
    )(x, g.reshape(1, d), mod3, mod3)


def _mm_act_kernel(a_ref, w_ref, o_ref, wbf_ref, *, act):
    @pl.when(pl.program_id(1) == 0)
    def _():
        wbf_ref[...] = w_ref[...].astype(BF16)

    acc = jnp.dot(a_ref[...], wbf_ref[...], preferred_element_type=F32)
    if act == "sigmoid":
        acc = _sigmoid(acc)
    o_ref[...] = acc.astype(o_ref.dtype)


def mm_act(a, w3, layer, n, act, out_dtype, dm, name):
    m, k = a.shape
    tm, tn = dm.tm_dense, dm.tn_wide
    return pl.pallas_call(
        functools.partial(_mm_act_kernel, act=act),
        grid=(n // tn, m // tm),
        in_specs=[pl.BlockSpec((tm, k), lambda j, i: (i, 0)),
                  pl.BlockSpec((None, k, tn), lambda j, i: (layer, 0, j))],
        out_specs=pl.BlockSpec((tm, tn), lambda j, i: (i, j)),
        out_shape=jax.ShapeDtypeStruct((m, n), out_dtype),
        scratch_shapes=[pltpu.VMEM((k, tn), BF16)],
        compiler_params=_cp("arbitrary", "arbitrary"),
        name=name,
    )(a, w3)


def _ifgate_kernel(a_ref, w_ref, wt_ref, bc_ref, br_ref, g_ref, gt_ref, *, hm):
    a = a_ref[...]
    gc = jnp.dot(a, w_ref[...].astype(BF16), preferred_element_type=F32) + bc_ref[...]
    lane = lax.broadcasted_iota(jnp.int32, gc.shape, 1)
    g_ref[...] = jnp.where((lane >= hm) & (lane < 2 * hm), _log_sigmoid(gc), gc)
    gt = lax.dot_general(wt_ref[...].astype(BF16), a, (((1,), (1,)), ((), ())),
                         preferred_element_type=F32) + br_ref[...]
    row = lax.broadcasted_iota(jnp.int32, gt.shape, 0)
    gt_ref[...] = jnp.where((row >= hm) & (row < 2 * hm), _log_sigmoid(gt), gt)


def if_gates(h, w_in, layer, i_bias, f_bias, dm):
    m, k = h.shape
    hm = dm.ml_heads
    tm = dm.tm
    tpb = dm.seq // tm
    c0 = dm.n_main
    wt = jnp.zeros((SUBLANES, k), F32).at[:2 * hm].set(w_in[layer, :, c0:c0 + 2 * hm].T)
    bias = jnp.concatenate([i_bias, f_bias]).astype(F32)
    bc = jnp.zeros((1, LANES), F32).at[0, :2 * hm].set(bias)
    br = jnp.zeros((SUBLANES, 1), F32).at[:2 * hm, 0].set(bias)
    return pl.pallas_call(
        functools.partial(_ifgate_kernel, hm=hm),
        grid=(m // tm,),
        in_specs=[pl.BlockSpec((tm, k), lambda i: (i, 0)),
                  pl.BlockSpec((None, k, LANES), lambda i: (layer, 0, c0 // LANES)),
                  pl.BlockSpec((SUBLANES, k), lambda i: (0, 0)),
                  pl.BlockSpec((1, LANES), lambda i: (0, 0)),
                  pl.BlockSpec((SUBLANES, 1), lambda i: (0, 0))],
        out_specs=[pl.BlockSpec((tm, LANES), lambda i: (i, 0)),
                   pl.BlockSpec((None, SUBLANES, tm), lambda i: (i // tpb, 0, i % tpb))],
        out_shape=[jax.ShapeDtypeStruct((m, LANES), F32),
                   jax.ShapeDtypeStruct((dm.batch, SUBLANES, dm.seq), F32)],
        compiler_params=_cp("parallel"),
        name="if_gates",
    )(h, w_in, wt, bc, br)


def _t5_bucket_np(rel):
    n = jnp.maximum(rel, 0)
    max_exact = REL_BUCKETS // 2
    large = max_exact + (jnp.log(jnp.maximum(n, 1).astype(F32) / max_exact)
                         / math.log(REL_MAX_DIST / max_exact)
                         * (REL_BUCKETS - max_exact)).astype(jnp.int32)
    large = jnp.minimum(large, REL_BUCKETS - 1)
    return jnp.where(n < max_exact, n, large)


def _bias_tile_kernel(tab_ref, idd_ref, ide_ref, d_ref, e_ref):
    h2 = pl.program_id(0)
    far = tab_ref[REL_BUCKETS - 1, h2]
    idd = idd_ref[...]
    ide = ide_ref[...]
    d = jnp.zeros(idd.shape, F32)
    e = jnp.zeros(ide.shape, F32)
    for b in range(REL_BUCKETS):
        t = tab_ref[b, h2] - far
        d = jnp.where(idd == b, t, d)
        e = jnp.where(ide == b, t, e)
    r = lax.broadcasted_iota(jnp.int32, idd.shape, 0)
    c = lax.broadcasted_iota(jnp.int32, idd.shape, 1)
    d_ref[...] = jnp.where(r <= c, d * LOG2E, NEG)
    e_ref[...] = e * LOG2E


def bias_tiles(rel_bias):
    nb, h2 = rel_bias.shape
    r = jnp.arange(LANES, dtype=jnp.int32)[:, None]
    c = jnp.arange(LANES, dtype=jnp.int32)[None, :]
    idd = _t5_bucket_np(c - r)
    ide = _t5_bucket_np(LANES + c - r)
    spec = pl.BlockSpec((LANES, LANES), lambda h: (0, 0))
    ospec = pl.BlockSpec((None, LANES, LANES), lambda h: (h, 0, 0))
    return pl.pallas_call(
        _bias_tile_kernel,
        grid=(h2,),
        in_specs=[pl.BlockSpec(memory_space=pltpu.SMEM), spec, spec],
        out_specs=[ospec, ospec],
        out_shape=[jax.ShapeDtypeStruct((h2, LANES, LANES), F32)] * 2,
        compiler_params=_cp("arbitrary"),
        name="bias_tiles",
    )(rel_bias.astype(F32), idd, ide)


V_ROWS = DA_DV + 16


def _attn_kernel(q_ref, k_ref, v_ref, d0_ref, e0_ref, lam_ref, subg_ref, gq_ref, gk_ref, o_ref,
                 qq_ref, kn_ref, vt_ref, m_ref, acc_ref, bd_ref, be_ref, *, tq, rc, lam_init):
    qi = pl.program_id(2)
    nsub = tq // LANES

    @pl.when(qi == 0)
    def _():
        for mp in range(2):
            for a in range(nsub):
                for b in range(nsub):
                    rows = pl.ds(b * LANES, LANES)
                    cols = pl.ds(mp * tq + a * LANES, LANES)
                    if a == b:
                        bd_ref[rows, cols] = d0_ref[mp]
                    elif a == b + 1:
                        bd_ref[rows, cols] = e0_ref[mp]
                    elif a < b:
                        bd_ref[rows, cols] = jnp.full((LANES, LANES), NEG, F32)
                    else:
                        bd_ref[rows, cols] = jnp.zeros((LANES, LANES), F32)
                    if a == 0 and b == nsub - 1:
                        be_ref[rows, cols] = e0_ref[mp]
                    else:
                        be_ref[rows, cols] = jnp.zeros((LANES, LANES), F32)

        def prep_kv(ci, carry):
            rows = pl.ds(pl.multiple_of(ci * tq, tq), tq)
            kf = k_ref[rows, :].astype(F32)
            low = lax.broadcasted_iota(jnp.int32, kf.shape, 1) < DA_DQK
            sq = kf * kf
            ss1 = jnp.sum(jnp.where(low, sq, 0.0), axis=1, keepdims=True)
            ss2 = jnp.sum(jnp.where(low, 0.0, sq), axis=1, keepdims=True)
            rs = lax.rsqrt(jnp.where(low, ss1, ss2) * (1.0 / DA_DQK) + EPS)
            kn_ref[rows, :] = (kf * rs * gk_ref[...]).astype(BF16)
            vt_ref[ci, pl.ds(0, DA_DV), :] = v_ref[rows, :].astype(F32).T.astype(BF16)
            pad_row = lax.broadcasted_iota(jnp.int32, (V_ROWS - DA_DV, tq), 0)
            vt_ref[ci, pl.ds(DA_DV, V_ROWS - DA_DV), :] = jnp.where(pad_row == 0, 1.0, 0.0).astype(BF16)
            return carry

        lax.fori_loop(0, k_ref.shape[0] // tq, prep_kv, 0)

    qf = q_ref[...].astype(F32)
    lane = lax.broadcasted_iota(jnp.int32, qf.shape, 1)
    for mp in range(2):
        qm = jnp.where((lane < DA_DQK) == (mp == 0), qf, 0.0)
        ss = jnp.sum(qm * qm, axis=1, keepdims=True)
        qn = qm * lax.rsqrt(ss * (1.0 / DA_DQK) + EPS) * gq_ref[...]
        qq_ref[pl.ds(mp * tq, tq), :] = qn.astype(BF16)
    m_ref[...] = jnp.full(m_ref.shape, NEG, F32)
    acc_ref[...] = jnp.zeros(acc_ref.shape, F32)

    nchunk = 2 * tq // rc

    def update(ki, bias_ref):
        k = kn_ref[pl.ds(pl.multiple_of(ki * tq, tq), tq), :]
        vt = vt_ref[ki]
        ss = [lax.dot_general(k, qq_ref[pl.ds(c * rc, rc), :], (((1,), (1,)), ((), ())),
                              preferred_element_type=F32) for c in range(nchunk)]
        m_prev_all = m_ref[...]
        results = []
        for c in range(nchunk):
            s = ss[c]
            if bias_ref is not None:
                s = s + bias_ref[:, pl.ds(c * rc, rc)]
            m_prev = m_prev_all[:, c * rc:(c + 1) * rc]
            m_new = jnp.maximum(m_prev, jnp.max(s, axis=0, keepdims=True))
            p = jnp.exp2(s - m_new).astype(BF16)
            pv = jnp.dot(vt, p, preferred_element_type=F32)
            results.append((m_new, jnp.exp2(m_prev - m_new), pv))
        for c, (m_new, alpha, pv) in enumerate(results):
            cols = pl.ds(c * rc, rc)
            acc_ref[:, cols] = alpha * acc_ref[:, cols] + pv
            m_ref[:, cols] = m_new

    def far_body(ki, carry):
        update(ki, None)
        return carry

    lax.fori_loop(0, jnp.maximum(qi - 1, 0), far_body, 0)

    @pl.when(qi >= 1)
    def _():
        update(qi - 1, be_ref)

    update(qi, bd_ref)

    acc = acc_ref[...]
    o = acc[:DA_DV] / acc[DA_DV:DA_DV + 1]
    lamv = lam_ref[...]
    lam = (jnp.exp(jnp.sum(lamv[0:1] * lamv[1:2], axis=1, keepdims=True))
           - jnp.exp(jnp.sum(lamv[2:3] * lamv[3:4], axis=1, keepdims=True)) + lam_init)
    od = o[:, :tq] - lam * o[:, tq:]
    ms = jnp.mean(od * od, axis=0, keepdims=True)
    y = od * lax.rsqrt(ms + EPS) * subg_ref[...] * (1.0 - lam_init)
    o_ref[...] = y.T.astype(o_ref.dtype)


def diff_attention(proj, d0, e0, lamv, sub_g, q_g, k_g, lam_init, dm):
    m = proj.shape[0]
    t, tq, h = dm.seq, dm.tq, dm.da_heads
    nq = t // tq
    gq = (jnp.tile(q_g.astype(F32), 2) * (DA_DQK ** -0.5 * LOG2E)).reshape(1, LANES)
    gk = jnp.tile(k_g.astype(F32), 2).reshape(1, LANES)
    vec = pl.BlockSpec((1, LANES), lambda b, hh, qi: (0, 0))
    return pl.pallas_call(
        functools.partial(_attn_kernel, tq=tq, rc=min(dm.attn_rc, 2 * tq), lam_init=lam_init),
        grid=(dm.batch, h, nq),
        in_specs=[pl.BlockSpec((tq, LANES), lambda b, hh, qi: (b * nq + qi, hh)),
                  pl.BlockSpec((t, LANES), lambda b, hh, qi: (b, h + hh)),
                  pl.BlockSpec((t, LANES), lambda b, hh, qi: (b, 2 * h + hh)),
                  pl.BlockSpec((2, LANES, LANES), lambda b, hh, qi: (hh, 0, 0)),
                  pl.BlockSpec((2, LANES, LANES), lambda b, hh, qi: (hh, 0, 0)),
                  pl.BlockSpec((4, DA_DQK), lambda b, hh, qi: (0, 0)),
                  pl.BlockSpec((DA_DV, 1), lambda b, hh, qi: (0, 0)),
                  vec, vec],
        out_specs=pl.BlockSpec((tq, LANES), lambda b, hh, qi: (b * nq + qi, hh)),
        out_shape=jax.ShapeDtypeStruct((m, dm.d_branch), BF16),
        scratch_shapes=[pltpu.VMEM((2 * tq, LANES), BF16),
                        pltpu.VMEM((t, LANES), BF16),
                        pltpu.VMEM((nq, V_ROWS, tq), BF16),
                        pltpu.VMEM((1, 2 * tq), F32),
                        pltpu.VMEM((V_ROWS, 2 * tq), F32),
                        pltpu.VMEM((tq, 2 * tq), F32),
                        pltpu.VMEM((tq, 2 * tq), F32)],
        compiler_params=_cp("parallel", "parallel", "arbitrary"),
        name="diff_attention",
    )(proj, proj, proj, d0, e0, lamv, sub_g.reshape(DA_DV, 1), gq, gk)


def _mlstm_kernel(q_ref, k_ref, v_ref, og_ref, g_ref, gt_ref, cwq_ref, cwk_ref,
                  cbq_ref, cbk_ref, ng_ref, o_ref,
                  qbuf_ref, kbuf_ref, ct_ref, n_ref, m_ref, *, L, hm):
    h = pl.program_id(1)
    c = pl.program_id(2)

    @pl.when(c == 0)
    def _():
        qbuf_ref[pl.ds(0, SUBLANES), :] = jnp.zeros((SUBLANES, ML_DQK), F32)
        kbuf_ref[pl.ds(0, SUBLANES), :] = jnp.zeros((SUBLANES, ML_DQK), F32)
        ct_ref[...] = jnp.zeros(ct_ref.shape, F32)
        n_ref[...] = jnp.zeros(n_ref.shape, F32)
        m_ref[...] = jnp.zeros(m_ref.shape, F32)

    def conv_silu(x_ref, buf_ref, w_ref, b_ref):
        buf_ref[pl.ds(SUBLANES, L), :] = x_ref[...].astype(F32)
        w = w_ref[...]
        y = b_ref[...] + w[CONV_W - 1:CONV_W] * buf_ref[pl.ds(SUBLANES, L), :]
        for s in range(1, CONV_W):
            y = y + w[CONV_W - 1 - s:CONV_W - s] * buf_ref[pl.ds(SUBLANES - s, L), :]
        tail = buf_ref[pl.ds(L, SUBLANES), :]
        buf_ref[pl.ds(0, SUBLANES), :] = tail
        return y * _sigmoid(y)

    q = conv_silu(q_ref, qbuf_ref, cwq_ref, cbq_ref)
    k = conv_silu(k_ref, kbuf_ref, cwk_ref, cbk_ref) * (ML_DQK ** -0.5)
    v = v_ref[...]

    g = g_ref[...]
    gt = gt_ref[...]
    ii = lax.broadcasted_iota(jnp.int32, (L, L), 0)
    jj = lax.broadcasted_iota(jnp.int32, (L, L), 1)
    tril = (jj <= ii).astype(F32)
    triu = (ii <= jj).astype(F32)
    bcum = jnp.dot(tril, g, preferred_element_type=F32, precision=lax.Precision.HIGHEST)
    btcum = jnp.dot(gt, triu, preferred_element_type=F32, precision=lax.Precision.HIGHEST)
    lane = lax.broadcasted_iota(jnp.int32, g.shape, 1)
    ig_col = jnp.sum(jnp.where(lane == h, g, 0.0), axis=1, keepdims=True)
    b_col = jnp.sum(jnp.where(lane == hm + h, bcum, 0.0), axis=1, keepdims=True)
    row = lax.broadcasted_iota(jnp.int32, gt.shape, 0)
    ig_row = jnp.sum(jnp.where(row == h, gt, 0.0), axis=0, keepdims=True)
    b_row = jnp.sum(jnp.where(row == hm + h, btcum, 0.0), axis=0, keepdims=True)

    m_old = m_ref[...]
    dlog = jnp.where(jj <= ii, b_col - b_row + ig_row, NEG)
    m_inter = b_col + m_old
    m_s = jnp.maximum(m_inter, jnp.max(dlog, axis=1, keepdims=True))
    dmat = jnp.exp(dlog - m_s)
    qb = q.astype(BF16)
    kb = k.astype(BF16)
    qk = lax.dot_general(qb, kb, (((1,), (1,)), ((), ())), preferred_element_type=F32)
    w = dmat * qk
    inter = jnp.exp(m_inter - m_s)
    ct = ct_ref[...]
    n_row = n_ref[...]
    num = (jnp.dot(w.astype(BF16), v, preferred_element_type=F32)
           + inter * jnp.dot(qb, ct.astype(BF16), preferred_element_type=F32))
    den = (jnp.sum(w, axis=1, keepdims=True)
           + inter * jnp.sum(q * n_row, axis=1, keepdims=True))
    hh = num / jnp.maximum(jnp.abs(den), jnp.exp(-m_s))

    m_new = m_s[L - 1:L]
    b_last = b_col[L - 1:L]
    w_end = jnp.exp(b_last - b_col + ig_col - m_new)
    decay = jnp.exp(b_last + m_old - m_new)
    kw = k * w_end
    ct_ref[...] = decay * ct + lax.dot_general(kw.astype(BF16), v, (((0,), (0,)), ((), ())),
                                               preferred_element_type=F32)
    n_ref[...] = decay * n_row + jnp.sum(kw, axis=0, keepdims=True)
    m_ref[...] = m_new

    ms = jnp.mean(hh * hh, axis=-1, keepdims=True)
    y = hh * lax.rsqrt(ms + EPS) * ng_ref[...] * _sigmoid(og_ref[...].astype(F32))
    o_ref[...] = y.astype(o_ref.dtype)


def mlstm_branch(proj, g, gt, conv_w, conv_b, norm_g, dm):
    m = proj.shape[0]
    t, L, hm, h = dm.seq, dm.ml_chunk, dm.ml_heads, dm.da_heads
    nc = t // L
    qc0 = 3 * h
    kc0 = 3 * h + hm
    vc0 = 4 * hm
    oc0 = 5 * hm
    cw = conv_w.astype(F32)
    cb = conv_b.astype(F32).reshape(1, -1)
    return pl.pallas_call(
        functools.partial(_mlstm_kernel, L=L, hm=hm),
        grid=(dm.batch, hm, nc),
        in_specs=[pl.BlockSpec((L, ML_DQK), lambda b, hh, c: (b * nc + c, qc0 + hh)),
                  pl.BlockSpec((L, ML_DQK), lambda b, hh, c: (b * nc + c, kc0 + hh)),
                  pl.BlockSpec((L, ML_DV), lambda b, hh, c: (b * nc + c, vc0 + hh)),
                  pl.BlockSpec((L, ML_DV), lambda b, hh, c: (b * nc + c, oc0 + hh)),
                  pl.BlockSpec((L, LANES), lambda b, hh, c: (b * nc + c, 0)),
                  pl.BlockSpec((None, SUBLANES, L), lambda b, hh, c: (b, 0, c)),
                  pl.BlockSpec((CONV_W, ML_DQK), lambda b, hh, c: (0, hh)),
                  pl.BlockSpec((CONV_W, ML_DQK), lambda b, hh, c: (0, hm + hh)),
                  pl.BlockSpec((1, ML_DQK), lambda b, hh, c: (0, hh)),
                  pl.BlockSpec((1, ML_DQK), lambda b, hh, c: (0, hm + hh)),
                  pl.BlockSpec((1, ML_DV), lambda b, hh, c: (0, hh))],
        out_specs=pl.BlockSpec((L, ML_DV), lambda b, hh, c: (b * nc + c, hh)),
        out_shape=jax.ShapeDtypeStruct((m, dm.d_branch), BF16),
        scratch_shapes=[pltpu.VMEM((L + SUBLANES, ML_DQK), F32),
                        pltpu.VMEM((L + SUBLANES, ML_DQK), F32),
                        pltpu.VMEM((ML_DQK, ML_DV), F32),
                        pltpu.VMEM((1, ML_DQK), F32),
                        pltpu.VMEM((1, 1), F32)],
        compiler_params=_cp("parallel", "parallel", "arbitrary"),
        name="mlstm_branch",
    )(proj, proj, proj, proj, g, gt, cw, cw, cb, cb, norm_g.reshape(1, -1))


def _merge_kernel(ya_ref, yb_ref, wa_ref, wb_ref, sa_ref, sb_ref, o_ref, wabf_ref, wbbf_ref):
    @pl.when(pl.program_id(1) == 0)
    def _():
        wabf_ref[...] = wa_ref[...].astype(BF16)
        wbbf_ref[...] = wb_ref[...].astype(BF16)

    pa = jnp.dot(ya_ref[...], wabf_ref[...], preferred_element_type=F32)
    pb = jnp.dot(yb_ref[...], wbbf_ref[...], preferred_element_type=F32)
    o_ref[...] = (sa_ref[...].astype(F32) * pa + sb_ref[...].astype(F32) * pb).astype(o_ref.dtype)


def merge_branches(ya, yb, wa, wb, layer, sg, dm):
    m, kb = ya.shape
    d = dm.d_model
    tm, tn = dm.tm_dense, dm.tn
    nj = d // tn
    return pl.pallas_call(
        _merge_kernel,
        grid=(nj, m // tm),
        in_specs=[pl.BlockSpec((tm, kb), lambda j, i: (i, 0)),
                  pl.BlockSpec((tm, kb), lambda j, i: (i, 0)),
                  pl.BlockSpec((None, kb, tn), lambda j, i: (layer, 0, j)),
                  pl.BlockSpec((None, kb, tn), lambda j, i: (layer, 0, j)),
                  pl.BlockSpec((tm, tn), lambda j, i: (i, j)),
                  pl.BlockSpec((tm, tn), lambda j, i: (i, nj + j))],
        out_specs=pl.BlockSpec((tm, tn), lambda j, i: (i, j)),
        out_shape=jax.ShapeDtypeStruct((m, d), BF16),
        scratch_shapes=[pltpu.VMEM((kb, tn), BF16), pltpu.VMEM((kb, tn), BF16)],
        compiler_params=_cp("arbitrary", "arbitrary"),
        name="merge_branches",
    )(ya, yb, wa, wb, sg, sg)


def _mm_res_kernel(a_ref, w_ref, x_ref, gate_ref, o_ref, wbf_ref):
    @pl.when(pl.program_id(1) == 0)
    def _():
        wbf_ref[...] = w_ref[...].astype(BF16)

    acc = jnp.dot(a_ref[...], wbf_ref[...], preferred_element_type=F32)
    o_ref[...] = x_ref[...] + gate_ref[...] * acc


def mm_residual(a, w3, layer, x, mod3, gate_idx, dm):
    m, k = a.shape
    d = x.shape[1]
    wide = k <= dm.d_model
    tm = dm.tm_dense if wide else dm.tm
    tn = dm.tn_wide if wide else dm.tn
    nj = d // tn
    tpb = dm.seq // tm
    return pl.pallas_call(
        _mm_res_kernel,
        grid=(nj, m // tm),
        in_specs=[pl.BlockSpec((tm, k), lambda j, i: (i, 0)),
                  pl.BlockSpec((None, k, tn), lambda j, i: (layer, 0, j)),
                  pl.BlockSpec((tm, tn), lambda j, i: (i, j)),
                  pl.BlockSpec((None, 1, tn), lambda j, i: (i // tpb, 0, gate_idx * nj + j))],
        out_specs=pl.BlockSpec((tm, tn), lambda j, i: (i, j)),
        out_shape=jax.ShapeDtypeStruct((m, d), F32),
        scratch_shapes=[pltpu.VMEM((k, tn), BF16)],
        compiler_params=_cp("arbitrary", "arbitrary"),
        name="mm_residual",
    )(a, w3, x, mod3)


def _swiglu(a, w1, w3):
    p1 = jnp.dot(a, w1, preferred_element_type=F32)
    p3 = jnp.dot(a, w3, preferred_element_type=F32)
    return (p1 * _sigmoid(p1) * p3).astype(BF16)


def _up_kernel(a_ref, w1_ref, w3_ref, o_ref, w1bf_ref, w3bf_ref):
    @pl.when(pl.program_id(1) == 0)
    def _():
        w1bf_ref[...] = w1_ref[...].astype(BF16)
        w3bf_ref[...] = w3_ref[...].astype(BF16)

    o_ref[...] = _swiglu(a_ref[...], w1bf_ref[...], w3bf_ref[...])


def ffn_up(a, w1, w3, layer, dm):
    m, k = a.shape
    f = w1.shape[-1]
    tm, tn = dm.tm_dense, dm.tn
    return pl.pallas_call(
        _up_kernel,
        grid=(f // tn, m // tm),
        in_specs=[pl.BlockSpec((tm, k), lambda j, i: (i, 0)),
                  pl.BlockSpec((None, k, tn), lambda j, i: (layer, 0, j)),
                  pl.BlockSpec((None, k, tn), lambda j, i: (layer, 0, j))],
        out_specs=pl.BlockSpec((tm, tn), lambda j, i: (i, j)),
        out_shape=jax.ShapeDtypeStruct((m, f), BF16),
        scratch_shapes=[pltpu.VMEM((k, tn), BF16), pltpu.VMEM((k, tn), BF16)],
        compiler_params=_cp("arbitrary", "arbitrary"),
        name="ffn_up",
    )(a, w1, w3)


def _gup_kernel(te_ref, tv_ref, a_ref, w1_ref, w3_ref, o_ref, w1bf_ref, w3bf_ref):
    i = pl.program_id(1)
    changed = (i == 0) | (te_ref[i] != te_ref[jnp.maximum(i - 1, 0)])

    @pl.when(changed)
    def _():
        w1bf_ref[...] = w1_ref[...].astype(BF16)
        w3bf_ref[...] = w3_ref[...].astype(BF16)

    @pl.when(tv_ref[i] == 1)
    def _():
        o_ref[...] = _swiglu(a_ref[...], w1bf_ref[...], w3bf_ref[...])

    @pl.when(tv_ref[i] == 0)
    def _():
        o_ref[...] = jnp.zeros(o_ref.shape, o_ref.dtype)


def moe_up(hs, w1, w3, lm, te, tv, dm):
    p, k = hs.shape
    f = w1.shape[-1]
    tg, tn = dm.tg, dm.tn
    grid_spec = pltpu.PrefetchScalarGridSpec(
        num_scalar_prefetch=2,
        grid=(f // tn, p // tg),
        in_specs=[pl.BlockSpec((tg, k), lambda j, i, te, tv: (i, 0)),
                  pl.BlockSpec((None, None, k, tn), lambda j, i, te, tv: (lm, te[i], 0, j)),
                  pl.BlockSpec((None, None, k, tn), lambda j, i, te, tv: (lm, te[i], 0, j))],
        out_specs=pl.BlockSpec((tg, tn), lambda j, i, te, tv: (i, j)),
        scratch_shapes=[pltpu.VMEM((k, tn), BF16), pltpu.VMEM((k, tn), BF16)])
    return pl.pallas_call(
        _gup_kernel,
        grid_spec=grid_spec,
        out_shape=jax.ShapeDtypeStruct((p, f), BF16),
        compiler_params=_cp("arbitrary", "arbitrary"),
        name="moe_up",
    )(te, tv, hs, w1, w3)


def _gdown_kernel(te_ref, tv_ref, u_ref, w_ref, o_ref, wbf_ref):
    i = pl.program_id(1)
    changed = (i == 0) | (te_ref[i] != te_ref[jnp.maximum(i - 1, 0)])

    @pl.when(changed)
    def _():
        wbf_ref[...] = w_ref[...].astype(BF16)

    @pl.when(tv_ref[i] == 1)
    def _():
        o_ref[...] = jnp.dot(u_ref[...], wbf_ref[...], preferred_element_type=F32)

    @pl.when(tv_ref[i] == 0)
    def _():
        o_ref[...] = jnp.zeros(o_ref.shape, o_ref.dtype)


def moe_down(u, w2, lm, te, tv, dm):
    p, f = u.shape
    d = w2.shape[-1]
    tg, tn = dm.tg, dm.tn
    grid_spec = pltpu.PrefetchScalarGridSpec(
        num_scalar_prefetch=2,
        grid=(d // tn, p // tg),
        in_specs=[pl.BlockSpec((tg, f), lambda j, i, te, tv: (i, 0)),
                  pl.BlockSpec((None, None, f, tn), lambda j, i, te, tv: (lm, te[i], 0, j))],
        out_specs=pl.BlockSpec((tg, tn), lambda j, i, te, tv: (i, j)),
        scratch_shapes=[pltpu.VMEM((f, tn), BF16)])
    return pl.pallas_call(
        _gdown_kernel,
        grid_spec=grid_spec,
        out_shape=jax.ShapeDtypeStruct((p, d), F32),
        compiler_params=_cp("arbitrary", "arbitrary"),
        name="moe_down",
    )(te, tv, u, w2)


def _router_kernel(a_ref, w_ref, o_ref, *, n_experts):
    logits = jnp.dot(a_ref[...].astype(BF16), w_ref[...].astype(BF16),
                     preferred_element_type=F32)
    lane = lax.broadcasted_iota(jnp.int32, logits.shape, 1)
    lg = jnp.where(lane < n_experts, logits, NEG)
    v1 = jnp.max(lg, axis=1, keepdims=True)
    i1 = jnp.min(jnp.where(lg == v1, lane, LANES), axis=1, keepdims=True)
    lg2 = jnp.where(lane == i1, NEG, lg)
    v2 = jnp.max(lg2, axis=1, keepdims=True)
    i2 = jnp.min(jnp.where(lg2 == v2, lane, LANES), axis=1, keepdims=True)
    e2 = jnp.exp(v2 - v1)
    g1 = 1.0 / (1.0 + e2)
    g2 = e2 / (1.0 + e2)
    out = jnp.where(lane == 0, g1, 0.0)
    out = jnp.where(lane == 1, g2, out)
    out = jnp.where(lane == 2, i1.astype(F32), out)
    out = jnp.where(lane == 3, i2.astype(F32), out)
    o_ref[...] = out


def router(hf, router_w, lm, dm):
    m, k = hf.shape
    e = dm.n_experts
    tm = dm.tm
    wpad = jnp.zeros((k, LANES), F32).at[:, :e].set(router_w[lm])
    return pl.pallas_call(
        functools.partial(_router_kernel, n_experts=e),
        grid=(m // tm,),
        in_specs=[pl.BlockSpec((tm, k), lambda i: (i, 0)),
                  pl.BlockSpec((k, LANES), lambda i: (0, 0))],
        out_specs=pl.BlockSpec((tm, LANES), lambda i: (i, 0)),
        out_shape=jax.ShapeDtypeStruct((m, LANES), F32),
        compiler_params=_cp("parallel"),
        name="router",
    )(hf, wpad)


def _row_copy(src_ref, dst_ref, src_row, dst_row, sem):
    return pltpu.make_async_copy(src_ref.at[pl.ds(src_row, 1), :],
                                 dst_ref.at[pl.ds(dst_row, 1), :], sem)


def _gather_kernel(idx_ref, src_ref, o_ref, buf_ref, sem, *, rows):
    def issue(r, carry):
        _row_copy(src_ref, buf_ref, idx_ref[0, 0, r], r, sem).start()
        return carry

    lax.fori_loop(0, rows, issue, 0, unroll=DMA_UNROLL)

    def wait(r, carry):
        _row_copy(src_ref, buf_ref, 0, r, sem).wait()
        return carry

    lax.fori_loop(0, rows, wait, 0, unroll=DMA_UNROLL)
    o_ref[...] = buf_ref[...].astype(o_ref.dtype)


def gather_rows(src, idx, rows, out_dtype):
    p = idx.shape[0]
    d = src.shape[1]
    nblk = p // rows
    return pl.pallas_call(
        functools.partial(_gather_kernel, rows=rows),
        grid=(nblk,),
        in_specs=[pl.BlockSpec((1, 1, rows), lambda i: (i, 0, 0), memory_space=pltpu.SMEM),
                  pl.BlockSpec(memory_space=pl.ANY)],
        out_specs=pl.BlockSpec((rows, d), lambda i: (i, 0)),
        out_shape=jax.ShapeDtypeStruct((p, d), out_dtype),
        scratch_shapes=[pltpu.VMEM((rows, d), src.dtype), pltpu.SemaphoreType.DMA(())],
        compiler_params=_cp("arbitrary"),
        name="gather_rows",
    )(idx.reshape(nblk, 1, rows), src)


def _combine_kernel(pos_ref, ys_ref, x_ref, r_ref, gate_ref, o_ref, buf_ref, sem, *, rows):
    def issue(r, carry):
        _row_copy(ys_ref, buf_ref.at[0], pos_ref[0, 0, 2 * r], r, sem).start()
        _row_copy(ys_ref, buf_ref.at[1], pos_ref[0, 0, 2 * r + 1], r, sem).start()
        return carry

    lax.fori_loop(0, rows, issue, 0, unroll=DMA_UNROLL)

    def wait(r, carry):
        _row_copy(ys_ref, buf_ref.at[0], 0, r, sem).wait()
        _row_copy(ys_ref, buf_ref.at[1], 0, r, sem).wait()
        return carry

    lax.fori_loop(0, rows, wait, 0, unroll=DMA_UNROLL)
    rr = r_ref[...]
    mix = rr[:, 0:1] * buf_ref[0] + rr[:, 1:2] * buf_ref[1]
    o_ref[...] = x_ref[...] + gate_ref[...] * mix


def moe_combine(ys, pos, x, rinfo, mod3, gate_idx, dm):
    m, d = x.shape
    rows = dm.tc
    nblk = m // rows
    tpb = dm.seq // rows
    return pl.pallas_call(
        functools.partial(_combine_kernel, rows=rows),
        grid=(nblk,),
        in_specs=[pl.BlockSpec((1, 1, 2 * rows), lambda i: (i, 0, 0), memory_space=pltpu.SMEM),
                  pl.BlockSpec(memory_space=pl.ANY),
                  pl.BlockSpec((rows, d), lambda i: (i, 0)),
                  pl.BlockSpec((rows, LANES), lambda i: (i, 0)),
                  pl.BlockSpec((None, 1, d), lambda i: (i // tpb, 0, gate_idx))],
        out_specs=pl.BlockSpec((rows, d), lambda i: (i, 0)),
        out_shape=jax.ShapeDtypeStruct((m, d), F32),
        scratch_shapes=[pltpu.VMEM((2, rows, d), F32), pltpu.SemaphoreType.DMA(())],
        compiler_params=_cp("arbitrary"),
        name="moe_combine",
    )(pos.reshape(nblk, 1, 2 * rows), ys, x, rinfo, mod3)


def routing_tables(rinfo, dm):
    m = rinfo.shape[0]
    e, tg = dm.n_experts, dm.tg
    n_tiles = (TOP_K * m) // tg + e
    eid = rinfo[:, 2:2 + TOP_K].astype(jnp.int32).reshape(-1)
    onehot = (eid[:, None] == jnp.arange(e, dtype=jnp.int32)[None, :]).astype(jnp.int32)
    csum = jnp.cumsum(onehot, axis=0)
    rank = jnp.sum((csum - onehot) * onehot, axis=1)
    counts = csum[-1]
    tiles_per = (counts + tg - 1) // tg
    tile_end = jnp.cumsum(tiles_per)
    tile_start = tile_end - tiles_per
    pos = tile_start[eid] * tg + rank
    token = jnp.arange(TOP_K * m, dtype=jnp.int32) // TOP_K
    slot_token = jnp.zeros((n_tiles * tg,), jnp.int32).at[pos].set(token)
    tidx = jnp.arange(n_tiles, dtype=jnp.int32)
    used = tile_end[-1]
    te_raw = jnp.sum((tidx[:, None] >= tile_end[None, :]).astype(jnp.int32), axis=1)
    last_e = jnp.sum((jnp.maximum(used - 1, 0) >= tile_end).astype(jnp.int32))
    te = jnp.where(tidx < used, te_raw, last_e).astype(jnp.int32)
    tv = (tidx < used).astype(jnp.int32)
    return pos.astype(jnp.int32), slot_token, te, tv


def forward(dm, x, c, rel_bias, w_mod, b_mod, norm_mix_g, norm_ff_g, w_in,
            da_q_g, da_k_g, da_lam_q1, da_lam_k1, da_lam_q2, da_lam_k2, da_sub_g,
            ml_conv_w, ml_conv_b, ml_i_bias, ml_f_bias, ml_norm_g,
            w_branch_a, w_branch_b, w_out,
            ffn_w1, ffn_w3, ffn_w2, router_w, moe_w1, moe_w3, moe_w2):
    b, t, d = x.shape
    m = b * t
    h = dm.da_heads
    xf = x.astype(F32).reshape(m, d)
    mod_all = adaln_mod(c, w_mod, b_mod)
    d0, e0 = bias_tiles(rel_bias)
    c_gate = dm.n_main + 2 * dm.ml_heads
    w_g = w_in[:, :, c_gate:]

    for l in range(dm.depth):
        mod3 = mod_all[l].reshape(b, 1, 6 * d)
        hb = norm_mod(xf, norm_mix_g[l], mod3, 1, 0, dm, BF16)
        proj = mm_act(hb, w_in, l, dm.n_main, None, BF16, dm, "proj_main")
        sg = mm_act(hb, w_g, l, 2 * d, "sigmoid", BF16, dm, "branch_gates")
        g, gt = if_gates(hb, w_in, l, ml_i_bias[l], ml_f_bias[l], dm)
        lam_init = 0.8 - 0.6 * math.exp(-0.3 * l)
        lamv = jnp.stack([da_lam_q1[l], da_lam_k1[l], da_lam_q2[l], da_lam_k2[l]]).astype(F32)
        ya = diff_attention(proj, d0, e0, lamv, da_sub_g[l].astype(F32), da_q_g[l], da_k_g[l],
                            lam_init, dm)
        yb = mlstm_branch(proj, g, gt, ml_conv_w[l], ml_conv_b[l], ml_norm_g[l].astype(F32), dm)
        merged = merge_branches(ya, yb, w_branch_a, w_branch_b, l, sg, dm)
        xf = mm_residual(merged, w_out, l, xf, mod3, 2, dm)
        if l % 2 == 0:
            hb = norm_mod(xf, norm_ff_g[l], mod3, 4, 3, dm, BF16)
            u = ffn_up(hb, ffn_w1, ffn_w3, l // 2, dm)
            xf = mm_residual(u, ffn_w2, l // 2, xf, mod3, 5, dm)
        else:
            lm = l // 2
            hf = norm_mod(xf, norm_ff_g[l], mod3, 4, 3, dm, F32)
            rinfo = router(hf, router_w, lm, dm)
            pos, slot_token, te, tv = routing_tables(rinfo, dm)
            hs = gather_rows(hf, slot_token, dm.tg, BF16)
            u = moe_up(hs, moe_w1, moe_w3, lm, te, tv, dm)
            ys = moe_down(u, moe_w2, lm, te, tv, dm)
            xf = moe_combine(ys, pos, xf, rinfo, mod3, 5, dm)
    return xf.reshape(b, t, d)


def kernel(x, c, rel_bias, w_mod, b_mod, norm_mix_g, norm_ff_g, w_in, da_q_g, da_k_g, da_lam_q1, da_lam_k1, da_lam_q2, da_lam_k2, da_sub_g, ml_conv_w, ml_conv_b, ml_i_bias, ml_f_bias, ml_norm_g, w_branch_a, w_branch_b, w_out, ffn_w1, ffn_w3, ffn_w2, router_w, moe_w1, moe_w3, moe_w2):
    b, t, d = x.shape
    dm = Dims(d_model=d, batch=b, seq=t, depth=w_mod.shape[0], d_ff=ffn_w1.shape[-1],
              n_experts=router_w.shape[-1])
    return forward(dm, x, c, rel_bias, w_mod, b_mod, norm_mix_g, norm_ff_g, w_in,
                   da_q_g, da_k_g, da_lam_q1, da_lam_k1, da_lam_q2, da_lam_k2, da_sub_g,
                   ml_conv_w, ml_conv_b, ml_i_bias, ml_f_bias, ml_norm_g,
                   w_branch_a, w_branch_b, w_out,
                   ffn_w1, ffn_w3, ffn_w2, router_w, moe_w1, moe_w3, moe_w2)
```

```python
import functools
import math
from typing import NamedTuple

import numpy as np
import jax
import jax.numpy as jnp
from jax import lax
from jax.experimental import pallas as pl
from jax.experimental.pallas import tpu as pltpu

F32 = jnp.float32
BF16 = jnp.bfloat16

DA_DQK = 64
DA_DV = 128
ML_DQK = 128
ML_DV = 256
CONV_W = 4
TOP_K = 2
REL_BUCKETS = 32
REL_MAX_DIST = 128
EPS = 1e-6
NEG = -1e30
LOG2E = math.log2(math.e)
LANES = 128
SUBLANES = 8
VMEM_LIMIT = 52 * 1024 * 1024
DMA_UNROLL = 8


class Dims(NamedTuple):
    d_model: int
    batch: int
    seq: int
    depth: int
    d_ff: int
    n_experts: int
    tm: int = 512
    tm_dense: int = 1024
    tn: int = 512
    tn_wide: int = 1024
    tq: int = 1024
    ml_chunk: int = 256
    tg: int = 512
    tc: int = 256
    attn_rc: int = 2048

    @property
    def d_branch(self):
        return self.d_model // 2

    @property
    def da_heads(self):
        return self.d_branch // DA_DV

    @property
    def ml_heads(self):
        return self.d_branch // ML_DV

    @property
    def n_main(self):
        return 3 * self.d_branch + 2 * self.ml_heads * ML_DQK + 2 * self.d_branch

    @property
    def tokens(self):
        return self.batch * self.seq


def _cp(*sem):
    return pltpu.CompilerParams(dimension_semantics=sem, vmem_limit_bytes=VMEM_LIMIT)


def _sigmoid(x):
    return 1.0 / (1.0 + jnp.exp(-x))


def _log_sigmoid(x):
    return jnp.minimum(x, 0.0) - jnp.log(1.0 + jnp.exp(-jnp.abs(x)))


def _mod_kernel(c_ref, w_ref, b_ref, o_ref):
    c = c_ref[...]
    ca = c * _sigmoid(c)
    o_ref[...] = jnp.dot(ca, w_ref[...], preferred_element_type=F32,
                         precision=lax.Precision.HIGHEST) + b_ref[...]


def adaln_mod(c, w_mod, b_mod):
    depth, d, n = w_mod.shape
    b = c.shape[0]
    bp = -(-b // SUBLANES) * SUBLANES
    cp = jnp.zeros((bp, d), F32).at[:b].set(c.astype(F32))
    tn = min(n, 1024)
    out = pl.pallas_call(
        _mod_kernel,
        grid=(depth, n // tn),
        in_specs=[pl.BlockSpec((bp, d), lambda l, j: (0, 0)),
                  pl.BlockSpec((None, d, tn), lambda l, j: (l, 0, j)),
                  pl.BlockSpec((None, 1, tn), lambda l, j: (l, 0, j))],
        out_specs=pl.BlockSpec((None, bp, tn), lambda l, j: (l, 0, j)),
        out_shape=jax.ShapeDtypeStruct((depth, bp, n), F32),
        compiler_params=_cp("parallel", "parallel"),
        name="adaln_mod",
    )(cp, w_mod, b_mod.reshape(depth, 1, n))
    return out[:, :b]


def _norm_mod_kernel(x_ref, g_ref, sc_ref, sh_ref, o_ref):
    x = x_ref[...]
    ms = jnp.mean(x * x, axis=-1, keepdims=True)
    h = x * lax.rsqrt(ms + EPS) * g_ref[...] * (1.0 + sc_ref[...]) + sh_ref[...]
    o_ref[...] = h.astype(o_ref.dtype)


def norm_mod(x, g, mod3, scale_idx, shift_idx, dm, out_dtype):
    m, d = x.shape
    tm = dm.tm
    tpb = dm.seq // tm
    return pl.pallas_call(
        _norm_mod_kernel,
        grid=(m // tm,),
        in_specs=[pl.BlockSpec((tm, d), lambda i: (i, 0)),
                  pl.BlockSpec((1, d), lambda i: (0, 0)),
                  pl.BlockSpec((None, 1, d), lambda i: (i // tpb, 0, scale_idx)),
                  pl.BlockSpec((None, 1, d), lambda i: (i // tpb, 0, shift_idx))],
        out_specs=pl.BlockSpec((tm, d), lambda i: (i, 0)),
        out_shape=jax.ShapeDtypeStruct((m, d), out_dtype),
        compiler_params=_cp("parallel"),
        name="norm_mod",
    )(x, g.reshape(1, d), mod3, mod3)


def _mm_act_kernel(a_ref, w_ref, o_ref, wbf_ref, *, act):
    @pl.when(pl.program_id(1) == 0)
    def _():
        wbf_ref[...] = w_ref[...].astype(BF16)

    acc = jnp.dot(a_ref[...], wbf_ref[...], preferred_element_type=F32)
    if act == "sigmoid":
        acc = _sigmoid(acc)
    o_ref[...] = acc.astype(o_ref.dtype)


def mm_act(a, w3, layer, n, act, out_dtype, dm, name):
    m, k = a.shape
    tm, tn = dm.tm_dense, dm.tn_wide
    return pl.pallas_call(
        functools.partial(_mm_act_kernel, act=act),
        grid=(n // tn, m // tm),
        in_specs=[pl.BlockSpec((tm, k), lambda j, i: (i, 0)),
                  pl.BlockSpec((None, k, tn), lambda j, i: (layer, 0, j))],
        out_specs=pl.BlockSpec((tm, tn), lambda j, i: (i, j)),
        out_shape=jax.ShapeDtypeStruct((m, n), out_dtype),
        scratch_shapes=[pltpu.VMEM((k, tn), BF16)],
        compiler_params=_cp("arbitrary", "arbitrary"),
        name=name,
    )(a, w3)


def _ifgate_kernel(a_ref, w_ref, wt_ref, bc_ref, br_ref, g_ref, gt_ref, *, hm):
    a = a_ref[...]
    gc = jnp.dot(a, w_ref[...].astype(BF16), preferred_element_type=F32) + bc_ref[...]
    lane = lax.broadcasted_iota(jnp.int32, gc.shape, 1)
    g_ref[...] = jnp.where((lane >= hm) & (lane < 2 * hm), _log_sigmoid(gc), gc)
    gt = lax.dot_general(wt_ref[...].astype(BF16), a, (((1,), (1,)), ((), ())),
                         preferred_element_type=F32) + br_ref[...]
    row = lax.broadcasted_iota(jnp.int32, gt.shape, 0)
    gt_ref[...] = jnp.where((row >= hm) & (row < 2 * hm), _log_sigmoid(gt), gt)


def if_gates(h, w_in, layer, i_bias, f_bias, dm):
    m, k = h.shape
    hm = dm.ml_heads
    tm = dm.tm
    tpb = dm.seq // tm
    c0 = dm.n_main
    wt = jnp.zeros((SUBLANES, k), F32).at[:2 * hm].set(w_in[layer, :, c0:c0 + 2 * hm].T)
    bias = jnp.concatenate([i_bias, f_bias]).astype(F32)
    bc = jnp.zeros((1, LANES), F32).at[0, :2 * hm].set(bias)
    br = jnp.zeros((SUBLANES, 1), F32).at[:2 * hm, 0].set(bias)
    return pl.pallas_call(
        functools.partial(_ifgate_kernel, hm=hm),
        grid=(m // tm,),
        in_specs=[pl.BlockSpec((tm, k), lambda i: (i, 0)),
                  pl.BlockSpec((None, k, LANES), lambda i: (layer, 0, c0 // LANES)),
                  pl.BlockSpec((SUBLANES, k), lambda i: (0, 0)),
                  pl.BlockSpec((1, LANES), lambda i: (0, 0)),
                  pl.BlockSpec((SUBLANES, 1), lambda i: (0, 0))],
        out_specs=[pl.BlockSpec((tm, LANES), lambda i: (i, 0)),
                   pl.BlockSpec((None, SUBLANES, tm), lambda i: (i // tpb, 0, i % tpb))],
        out_shape=[jax.ShapeDtypeStruct((m, LANES), F32),
                   jax.ShapeDtypeStruct((dm.batch, SUBLANES, dm.seq), F32)],
        compiler_params=_cp("parallel"),
        name="if_gates",
    )(h, w_in, wt, bc, br)


def _t5_bucket_np(rel):
    n = jnp.maximum(rel, 0)
    max_exact = REL_BUCKETS // 2
    large = max_exact + (jnp.log(jnp.maximum(n, 1).astype(F32) / max_exact)
                         / math.log(REL_MAX_DIST / max_exact)
                         * (REL_BUCKETS - max_exact)).astype(jnp.int32)
    large = jnp.minimum(large, REL_BUCKETS - 1)
    return jnp.where(n < max_exact, n, large)


def _bias_tile_kernel(tab_ref, idd_ref, ide_ref, d_ref, e_ref):
    h2 = pl.program_id(0)
    far = tab_ref[REL_BUCKETS - 1, h2]
    idd = idd_ref[...]
    ide = ide_ref[...]
    d = jnp.zeros(idd.shape, F32)
    e = jnp.zeros(ide.shape, F32)
    for b in range(REL_BUCKETS):
        t = tab_ref[b, h2] - far
        d = jnp.where(idd == b, t, d)
        e = jnp.where(ide == b, t, e)
    r = lax.broadcasted_iota(jnp.int32, idd.shape, 0)
    c = lax.broadcasted_iota(jnp.int32, idd.shape, 1)
    d_ref[...] = jnp.where(r <= c, d * LOG2E, NEG)
    e_ref[...] = e * LOG2E


def bias_tiles(rel_bias):
    nb, h2 = rel_bias.shape
    r = jnp.arange(LANES, dtype=jnp.int32)[:, None]
    c = jnp.arange(LANES, dtype=jnp.int32)[None, :]
    idd = _t5_bucket_np(c - r)
    ide = _t5_bucket_np(LANES + c - r)
    spec = pl.BlockSpec((LANES, LANES), lambda h: (0, 0))
    ospec = pl.BlockSpec((None, LANES, LANES), lambda h: (h, 0, 0))
    return pl.pallas_call(
        _bias_tile_kernel,
        grid=(h2,),
        in_specs=[pl.BlockSpec(memory_space=pltpu.SMEM), spec, spec],
        out_specs=[ospec, ospec],
        out_shape=[jax.ShapeDtypeStruct((h2, LANES, LANES), F32)] * 2,
        compiler_params=_cp("arbitrary"),
        name="bias_tiles",
    )(rel_bias.astype(F32), idd, ide)


V_ROWS = DA_DV + 16
BOUND_MARGIN = 1.02
MAX_BOUND_SPREAD = 100.0


def _attn_kernel(q_ref, k_ref, v_ref, d0_ref, e0_ref, lam_ref, subg_ref, gq_ref, gk_ref, o_ref,
                 qq_ref, kn_ref, vt_ref, m_ref, acc_ref, bd_ref, be_ref, stat_ref,
                 *, tq, rc, lam_init):
    qi = pl.program_id(2)
    nsub = tq // LANES

    @pl.when(qi == 0)
    def _():
        for mp in range(2):
            for a in range(nsub):
                for b in range(nsub):
                    rows = pl.ds(b * LANES, LANES)
                    cols = pl.ds(mp * tq + a * LANES, LANES)
                    if a == b:
                        bd_ref[rows, cols] = d0_ref[mp]
                    elif a == b + 1:
                        bd_ref[rows, cols] = e0_ref[mp]
                    elif a < b:
                        bd_ref[rows, cols] = jnp.full((LANES, LANES), NEG, F32)
                    else:
                        bd_ref[rows, cols] = jnp.zeros((LANES, LANES), F32)
                    if a == 0 and b == nsub - 1:
                        be_ref[rows, cols] = e0_ref[mp]
                    else:
                        be_ref[rows, cols] = jnp.zeros((LANES, LANES), F32)

        def prep_kv(ci, carry):
            rows = pl.ds(pl.multiple_of(ci * tq, tq), tq)
            kf = k_ref[rows, :].astype(F32)
            low = lax.broadcasted_iota(jnp.int32, kf.shape, 1) < DA_DQK
            sq = kf * kf
            ss1 = jnp.sum(jnp.where(low, sq, 0.0), axis=1, keepdims=True)
            ss2 = jnp.sum(jnp.where(low, 0.0, sq), axis=1, keepdims=True)
            rs = lax.rsqrt(jnp.where(low, ss1, ss2) * (1.0 / DA_DQK) + EPS)
            kn = (kf * rs * gk_ref[...]).astype(BF16)
            kn_ref[rows, :] = kn
            kq = kn.astype(F32) * kn.astype(F32)
            n1 = jnp.sum(jnp.where(low, kq, 0.0), axis=1, keepdims=True)
            n2 = jnp.sum(jnp.where(low, 0.0, kq), axis=1, keepdims=True)
            vt_ref[ci, pl.ds(0, DA_DV), :] = v_ref[rows, :].astype(F32).T.astype(BF16)
            pad_row = lax.broadcasted_iota(jnp.int32, (V_ROWS - DA_DV, tq), 0)
            vt_ref[ci, pl.ds(DA_DV, V_ROWS - DA_DV), :] = jnp.where(pad_row == 0, 1.0, 0.0).astype(BF16)
            return jnp.maximum(carry, jnp.max(jnp.maximum(n1, n2), axis=0, keepdims=True))

        ksq_max = lax.fori_loop(0, k_ref.shape[0] // tq, prep_kv, jnp.zeros((1, 1), F32))
        d0v = d0_ref[...]
        e0v = e0_ref[...]
        valid = d0v > 0.5 * NEG
        b_hi = jnp.maximum(jnp.maximum(jnp.max(jnp.where(valid, d0v, 0.0)), jnp.max(e0v)), 0.0)
        b_lo = jnp.minimum(jnp.minimum(jnp.min(jnp.where(valid, d0v, 0.0)), jnp.min(e0v)), 0.0)
        stat_ref[0:1, :] = jnp.broadcast_to(jnp.sqrt(ksq_max), (1, LANES))
        stat_ref[1:2, :] = jnp.full((1, LANES), b_hi, F32)
        stat_ref[2:3, :] = jnp.full((1, LANES), b_hi - b_lo, F32)

    qf = q_ref[...].astype(F32)
    lane = lax.broadcasted_iota(jnp.int32, qf.shape, 1)
    for mp in range(2):
        qm = jnp.where((lane < DA_DQK) == (mp == 0), qf, 0.0)
        ss = jnp.sum(qm * qm, axis=1, keepdims=True)
        qn = qm * lax.rsqrt(ss * (1.0 / DA_DQK) + EPS) * gq_ref[...]
        qq_ref[pl.ds(mp * tq, tq), :] = qn.astype(BF16)
    acc_ref[...] = jnp.zeros(acc_ref.shape, F32)

    nchunk = 2 * tq // rc

    qq = qq_ref[...].astype(F32)
    qn2 = lax.dot_general(jnp.ones((SUBLANES, LANES), BF16), (qq * qq).astype(BF16),
                          (((1,), (1,)), ((), ())), preferred_element_type=F32)[0:1]
    m_bound = jnp.sqrt(qn2) * (BOUND_MARGIN * stat_ref[0:1, 0:1]) + stat_ref[1:2, 0:1]
    spread = 2.0 * jnp.max(m_bound, axis=1, keepdims=True) + stat_ref[2:3, 0:1]
    bounded = spread[0, 0] <= MAX_BOUND_SPREAD

    def update(ki, bias_ref, online):
        k = kn_ref[pl.ds(pl.multiple_of(ki * tq, tq), tq), :]
        vt = vt_ref[ki]
        ss = [lax.dot_general(k, qq_ref[pl.ds(c * rc, rc), :], (((1,), (1,)), ((), ())),
                              preferred_element_type=F32) for c in range(nchunk)]
        m_prev_all = m_ref[...]
        results = []
        for c in range(nchunk):
            s = ss[c]
            if bias_ref is not None:
                s = s + bias_ref[:, pl.ds(c * rc, rc)]
            m_prev = m_prev_all[:, c * rc:(c + 1) * rc]
            m_new = jnp.maximum(m_prev, jnp.max(s, axis=0, keepdims=True)) if online else m_prev
            p = jnp.exp2(s - m_new).astype(BF16)
            pv = jnp.dot(vt, p, preferred_element_type=F32)
            results.append((m_new, jnp.exp2(m_prev - m_new) if online else None, pv))
        for c, (m_new, alpha, pv) in enumerate(results):
            cols = pl.ds(c * rc, rc)
            if online:
                acc_ref[:, cols] = alpha * acc_ref[:, cols] + pv
                m_ref[:, cols] = m_new
            else:
                acc_ref[:, cols] = acc_ref[:, cols] + pv

    def attend(online):
        def far_body(ki, carry):
            update(ki, None, online)
            return carry

        lax.fori_loop(0, jnp.maximum(qi - 1, 0), far_body, 0)

        @pl.when(qi >= 1)
        def _():
            update(qi - 1, be_ref, online)

        update(qi, bd_ref, online)

    @pl.when(bounded)
    def _():
        m_ref[...] = m_bound
        attend(False)

    @pl.when(jnp.logical_not(bounded))
    def _():
        m_ref[...] = jnp.full(m_ref.shape, NEG, F32)
        attend(True)

    acc = acc_ref[...]
    o = acc[:DA_DV] / acc[DA_DV:DA_DV + 1]
    lamv = lam_ref[...]
    lam = (jnp.exp(jnp.sum(lamv[0:1] * lamv[1:2], axis=1, keepdims=True))
           - jnp.exp(jnp.sum(lamv[2:3] * lamv[3:4], axis=1, keepdims=True)) + lam_init)
    od = o[:, :tq] - lam * o[:, tq:]
    ms = jnp.mean(od * od, axis=0, keepdims=True)
    y = od * lax.rsqrt(ms + EPS) * subg_ref[...] * (1.0 - lam_init)
    o_ref[...] = y.T.astype(o_ref.dtype)


def diff_attention(proj, d0, e0, lamv, sub_g, q_g, k_g, lam_init, dm):
    m = proj.shape[0]
    t, tq, h = dm.seq, dm.tq, dm.da_heads
    nq = t // tq
    gq = (jnp.tile(q_g.astype(F32), 2) * (DA_DQK ** -0.5 * LOG2E)).reshape(1, LANES)
    gk = jnp.tile(k_g.astype(F32), 2).reshape(1, LANES)
    vec = pl.BlockSpec((1, LANES), lambda b, hh, qi: (0, 0))
    return pl.pallas_call(
        functools.partial(_attn_kernel, tq=tq, rc=min(dm.attn_rc, 2 * tq), lam_init=lam_init),
        grid=(dm.batch, h, nq),
        in_specs=[pl.BlockSpec((tq, LANES), lambda b, hh, qi: (b * nq + qi, hh)),
                  pl.BlockSpec((t, LANES), lambda b, hh, qi: (b, h + hh)),
                  pl.BlockSpec((t, LANES), lambda b, hh, qi: (b, 2 * h + hh)),
                  pl.BlockSpec((2, LANES, LANES), lambda b, hh, qi: (hh, 0, 0)),
                  pl.BlockSpec((2, LANES, LANES), lambda b, hh, qi: (hh, 0, 0)),
                  pl.BlockSpec((4, DA_DQK), lambda b, hh, qi: (0, 0)),
                  pl.BlockSpec((DA_DV, 1), lambda b, hh, qi: (0, 0)),
                  vec, vec],
        out_specs=pl.BlockSpec((tq, LANES), lambda b, hh, qi: (b * nq + qi, hh)),
        out_shape=jax.ShapeDtypeStruct((m, dm.d_branch), BF16),
        scratch_shapes=[pltpu.VMEM((2 * tq, LANES), BF16),
                        pltpu.VMEM((t, LANES), BF16),
                        pltpu.VMEM((nq, V_ROWS, tq), BF16),
                        pltpu.VMEM((1, 2 * tq), F32),
                        pltpu.VMEM((V_ROWS, 2 * tq), F32),
                        pltpu.VMEM((tq, 2 * tq), F32),
                        pltpu.VMEM((tq, 2 * tq), F32),
                        pltpu.VMEM((SUBLANES, LANES), F32)],
        compiler_params=_cp("parallel", "parallel", "arbitrary"),
        name="diff_attention",
    )(proj, proj, proj, d0, e0, lamv, sub_g.reshape(DA_DV, 1), gq, gk)


def _mlstm_kernel(q_ref, k_ref, v_ref, og_ref, g_ref, gt_ref, cwq_ref, cwk_ref,
                  cbq_ref, cbk_ref, ng_ref, o_ref,
                  qbuf_ref, kbuf_ref, ct_ref, n_ref, m_ref, *, L, hm):
    c = pl.program_id(1)

    @pl.when(c == 0)
    def _():
        qbuf_ref[pl.ds(0, SUBLANES), :] = jnp.zeros((SUBLANES, hm * ML_DQK), F32)
        kbuf_ref[pl.ds(0, SUBLANES), :] = jnp.zeros((SUBLANES, hm * ML_DQK), F32)
        ct_ref[...] = jnp.zeros(ct_ref.shape, F32)
        n_ref[...] = jnp.zeros(n_ref.shape, F32)
        m_ref[...] = jnp.zeros(m_ref.shape, F32)

    def conv_silu(x_ref, buf_ref, w_ref, b_ref):
        buf_ref[pl.ds(SUBLANES, L), :] = x_ref[...].astype(F32)
        w = w_ref[...]
        y = b_ref[...] + w[CONV_W - 1:CONV_W] * buf_ref[pl.ds(SUBLANES, L), :]
        for s in range(1, CONV_W):
            y = y + w[CONV_W - 1 - s:CONV_W - s] * buf_ref[pl.ds(SUBLANES - s, L), :]
        tail = buf_ref[pl.ds(L, SUBLANES), :]
        buf_ref[pl.ds(0, SUBLANES), :] = tail
        return y * _sigmoid(y)

    q_all = conv_silu(q_ref, qbuf_ref, cwq_ref, cbq_ref)
    k_all = conv_silu(k_ref, kbuf_ref, cwk_ref, cbk_ref) * (ML_DQK ** -0.5)
    v_all = v_ref[...]
    og_all = og_ref[...]
    ng_all = ng_ref[...]

    g = g_ref[...]
    gt = gt_ref[...]
    ii = lax.broadcasted_iota(jnp.int32, (L, L), 0)
    jj = lax.broadcasted_iota(jnp.int32, (L, L), 1)
    tril = (jj <= ii).astype(F32)
    triu = (ii <= jj).astype(F32)
    bcum = jnp.dot(tril, g, preferred_element_type=F32, precision=lax.Precision.HIGHEST)
    btcum = jnp.dot(gt, triu, preferred_element_type=F32, precision=lax.Precision.HIGHEST)
    lane = lax.broadcasted_iota(jnp.int32, g.shape, 1)
    row = lax.broadcasted_iota(jnp.int32, gt.shape, 0)
    ct_all = ct_ref[...]
    n_all = n_ref[...]
    m_all = m_ref[...]

    new_state = []
    for h in range(hm):
        q = q_all[:, h * ML_DQK:(h + 1) * ML_DQK]
        k = k_all[:, h * ML_DQK:(h + 1) * ML_DQK]
        v = v_all[:, h * ML_DV:(h + 1) * ML_DV]
        ig_col = jnp.sum(jnp.where(lane == h, g, 0.0), axis=1, keepdims=True)
        b_col = jnp.sum(jnp.where(lane == hm + h, bcum, 0.0), axis=1, keepdims=True)
        ig_row = jnp.sum(jnp.where(row == h, gt, 0.0), axis=0, keepdims=True)
        b_row = jnp.sum(jnp.where(row == hm + h, btcum, 0.0), axis=0, keepdims=True)

        m_old = m_all[h]
        dlog = jnp.where(jj <= ii, b_col - b_row + ig_row, NEG)
        m_inter = b_col + m_old
        m_s = jnp.maximum(m_inter, jnp.max(dlog, axis=1, keepdims=True))
        dmat = jnp.exp(dlog - m_s)
        qb = q.astype(BF16)
        kb = k.astype(BF16)
        qk = lax.dot_general(qb, kb, (((1,), (1,)), ((), ())), preferred_element_type=F32)
        w = dmat * qk
        inter = jnp.exp(m_inter - m_s)
        ct = ct_all[h]
        n_row = n_all[h]
        num = (jnp.dot(w.astype(BF16), v, preferred_element_type=F32)
               + inter * jnp.dot(qb, ct.astype(BF16), preferred_element_type=F32))
        den = (jnp.sum(w, axis=1, keepdims=True)
               + inter * jnp.sum(q * n_row, axis=1, keepdims=True))
        hh = num / jnp.maximum(jnp.abs(den), jnp.exp(-m_s))

        m_new = m_s[L - 1:L]
        b_last = b_col[L - 1:L]
        w_end = jnp.exp(b_last - b_col + ig_col - m_new)
        decay = jnp.exp(b_last + m_old - m_new)
        kw = k * w_end
        ct_new = decay * ct + lax.dot_general(kw.astype(BF16), v, (((0,), (0,)), ((), ())),
                                              preferred_element_type=F32)
        n_new = decay * n_row + jnp.sum(kw, axis=0, keepdims=True)

        ms = jnp.mean(hh * hh, axis=-1, keepdims=True)
        cols = slice(h * ML_DV, (h + 1) * ML_DV)
        y = hh * lax.rsqrt(ms + EPS) * ng_all[:, cols] * _sigmoid(og_all[:, cols].astype(F32))
        new_state.append((ct_new, n_new, m_new, y))

    for h, (ct_new, n_new, m_new, y) in enumerate(new_state):
        ct_ref[h] = ct_new
        n_ref[h] = n_new
        m_ref[h] = m_new
        o_ref[:, pl.ds(h * ML_DV, ML_DV)] = y.astype(o_ref.dtype)


def mlstm_branch(proj, g, gt, conv_w, conv_b, norm_g, dm):
    m = proj.shape[0]
    t, L, hm, h = dm.seq, dm.ml_chunk, dm.ml_heads, dm.da_heads
    nc = t // L
    wq, wv = hm * ML_DQK, hm * ML_DV
    qc, kc = 3 * dm.d_branch // wq, 3 * dm.d_branch // wq + 1
    vc, oc = (3 * dm.d_branch + 2 * wq) // wv, (3 * dm.d_branch + 2 * wq) // wv + 1
    cw = conv_w.astype(F32)
    cb = conv_b.astype(F32).reshape(1, -1)
    return pl.pallas_call(
        functools.partial(_mlstm_kernel, L=L, hm=hm),
        grid=(dm.batch, nc),
        in_specs=[pl.BlockSpec((L, wq), lambda b, c: (b * nc + c, qc)),
                  pl.BlockSpec((L, wq), lambda b, c: (b * nc + c, kc)),
                  pl.BlockSpec((L, wv), lambda b, c: (b * nc + c, vc)),
                  pl.BlockSpec((L, wv), lambda b, c: (b * nc + c, oc)),
                  pl.BlockSpec((L, LANES), lambda b, c: (b * nc + c, 0)),
                  pl.BlockSpec((None, SUBLANES, L), lambda b, c: (b, 0, c)),
                  pl.BlockSpec((CONV_W, wq), lambda b, c: (0, 0)),
                  pl.BlockSpec((CONV_W, wq), lambda b, c: (0, 1)),
                  pl.BlockSpec((1, wq), lambda b, c: (0, 0)),
                  pl.BlockSpec((1, wq), lambda b, c: (0, 1)),
                  pl.BlockSpec((1, wv), lambda b, c: (0, 0))],
        out_specs=pl.BlockSpec((L, wv), lambda b, c: (b * nc + c, 0)),
        out_shape=jax.ShapeDtypeStruct((m, dm.d_branch), BF16),
        scratch_shapes=[pltpu.VMEM((L + SUBLANES, wq), F32),
                        pltpu.VMEM((L + SUBLANES, wq), F32),
                        pltpu.VMEM((hm, ML_DQK, ML_DV), F32),
                        pltpu.VMEM((hm, 1, ML_DQK), F32),
                        pltpu.VMEM((hm, 1, 1), F32)],
        compiler_params=_cp("parallel", "arbitrary"),
        name="mlstm_branch",
    )(proj, proj, proj, proj, g, gt, cw, cw, cb, cb, norm_g.reshape(1, -1))


def _merge_kernel(ya_ref, yb_ref, wa_ref, wb_ref, sa_ref, sb_ref, o_ref, wabf_ref, wbbf_ref):
    @pl.when(pl.program_id(1) == 0)
    def _():
        wabf_ref[...] = wa_ref[...].astype(BF16)
        wbbf_ref[...] = wb_ref[...].astype(BF16)

    pa = jnp.dot(ya_ref[...], wabf_ref[...], preferred_element_type=F32)
    pb = jnp.dot(yb_ref[...], wbbf_ref[...], preferred_element_type=F32)
    o_ref[...] = (sa_ref[...].astype(F32) * pa + sb_ref[...].astype(F32) * pb).astype(o_ref.dtype)


def merge_branches(ya, yb, wa, wb, layer, sg, dm):
    m, kb = ya.shape
    d = dm.d_model
    tm, tn = dm.tm_dense, dm.tn
    nj = d // tn
    return pl.pallas_call(
        _merge_kernel,
        grid=(nj, m // tm),
        in_specs=[pl.BlockSpec((tm, kb), lambda j, i: (i, 0)),
                  pl.BlockSpec((tm, kb), lambda j, i: (i, 0)),
                  pl.BlockSpec((None, kb, tn), lambda j, i: (layer, 0, j)),
                  pl.BlockSpec((None, kb, tn), lambda j, i: (layer, 0, j)),
                  pl.BlockSpec((tm, tn), lambda j, i: (i, j)),
                  pl.BlockSpec((tm, tn), lambda j, i: (i, nj + j))],
        out_specs=pl.BlockSpec((tm, tn), lambda j, i: (i, j)),
        out_shape=jax.ShapeDtypeStruct((m, d), BF16),
        scratch_shapes=[pltpu.VMEM((kb, tn), BF16), pltpu.VMEM((kb, tn), BF16)],
        compiler_params=_cp("arbitrary", "arbitrary"),
        name="merge_branches",
    )(ya, yb, wa, wb, sg, sg)


def _mm_res_kernel(a_ref, w_ref, x_ref, gate_ref, o_ref, wbf_ref):
    @pl.when(pl.program_id(1) == 0)
    def _():
        wbf_ref[...] = w_ref[...].astype(BF16)

    acc = jnp.dot(a_ref[...], wbf_ref[...], preferred_element_type=F32)
    o_ref[...] = x_ref[...] + gate_ref[...] * acc


def mm_residual(a, w3, layer, x, mod3, gate_idx, dm):
    m, k = a.shape
    d = x.shape[1]
    wide = k <= dm.d_model
    tm = dm.tm_dense if wide else dm.tm
    tn = dm.tn_wide if wide else dm.tn
    nj = d // tn
    tpb = dm.seq // tm
    return pl.pallas_call(
        _mm_res_kernel,
        grid=(nj, m // tm),
        in_specs=[pl.BlockSpec((tm, k), lambda j, i: (i, 0)),
                  pl.BlockSpec((None, k, tn), lambda j, i: (layer, 0, j)),
                  pl.BlockSpec((tm, tn), lambda j, i: (i, j)),
                  pl.BlockSpec((None, 1, tn), lambda j, i: (i // tpb, 0, gate_idx * nj + j))],
        out_specs=pl.BlockSpec((tm, tn), lambda j, i: (i, j)),
        out_shape=jax.ShapeDtypeStruct((m, d), F32),
        scratch_shapes=[pltpu.VMEM((k, tn), BF16)],
        compiler_params=_cp("arbitrary", "arbitrary"),
        name="mm_residual",
    )(a, w3, x, mod3)


def _swiglu(a, w1, w3):
    p1 = jnp.dot(a, w1, preferred_element_type=F32)
    p3 = jnp.dot(a, w3, preferred_element_type=F32)
    return (p1 * _sigmoid(p1) * p3).astype(BF16)


def _up_kernel(a_ref, w1_ref, w3_ref, o_ref, w1bf_ref, w3bf_ref):
    @pl.when(pl.program_id(1) == 0)
    def _():
        w1bf_ref[...] = w1_ref[...].astype(BF16)
        w3bf_ref[...] = w3_ref[...].astype(BF16)

    o_ref[...] = _swiglu(a_ref[...], w1bf_ref[...], w3bf_ref[...])


def ffn_up(a, w1, w3, layer, dm):
    m, k = a.shape
    f = w1.shape[-1]
    tm, tn = dm.tm_dense, dm.tn
    return pl.pallas_call(
        _up_kernel,
        grid=(f // tn, m // tm),
        in_specs=[pl.BlockSpec((tm, k), lambda j, i: (i, 0)),
                  pl.BlockSpec((None, k, tn), lambda j, i: (layer, 0, j)),
                  pl.BlockSpec((None, k, tn), lambda j, i: (layer, 0, j))],
        out_specs=pl.BlockSpec((tm, tn), lambda j, i: (i, j)),
        out_shape=jax.ShapeDtypeStruct((m, f), BF16),
        scratch_shapes=[pltpu.VMEM((k, tn), BF16), pltpu.VMEM((k, tn), BF16)],
        compiler_params=_cp("arbitrary", "arbitrary"),
        name="ffn_up",
    )(a, w1, w3)


def _gup_kernel(te_ref, tv_ref, a_ref, w1_ref, w3_ref, o_ref, w1bf_ref, w3bf_ref):
    i = pl.program_id(1)
    changed = (i == 0) | (te_ref[i] != te_ref[jnp.maximum(i - 1, 0)])

    @pl.when(changed)
    def _():
        w1bf_ref[...] = w1_ref[...].astype(BF16)
        w3bf_ref[...] = w3_ref[...].astype(BF16)

    @pl.when(tv_ref[i] == 1)
    def _():
        o_ref[...] = _swiglu(a_ref[...], w1bf_ref[...], w3bf_ref[...])

    @pl.when(tv_ref[i] == 0)
    def _():
        o_ref[...] = jnp.zeros(o_ref.shape, o_ref.dtype)


def moe_up(hs, w1, w3, lm, te, tv, dm):
    p, k = hs.shape
    f = w1.shape[-1]
    tg, tn = dm.tg, dm.tn
    grid_spec = pltpu.PrefetchScalarGridSpec(
        num_scalar_prefetch=2,
        grid=(f // tn, p // tg),
        in_specs=[pl.BlockSpec((tg, k), lambda j, i, te, tv: (i, 0)),
                  pl.BlockSpec((None, None, k, tn), lambda j, i, te, tv: (lm, te[i], 0, j)),
                  pl.BlockSpec((None, None, k, tn), lambda j, i, te, tv: (lm, te[i], 0, j))],
        out_specs=pl.BlockSpec((tg, tn), lambda j, i, te, tv: (i, j)),
        scratch_shapes=[pltpu.VMEM((k, tn), BF16), pltpu.VMEM((k, tn), BF16)])
    return pl.pallas_call(
        _gup_kernel,
        grid_spec=grid_spec,
        out_shape=jax.ShapeDtypeStruct((p, f), BF16),
        compiler_params=_cp("arbitrary", "arbitrary"),
        name="moe_up",
    )(te, tv, hs, w1, w3)


def _gdown_kernel(te_ref, tv_ref, u_ref, w_ref, o_ref, wbf_ref):
    i = pl.program_id(1)
    changed = (i == 0) | (te_ref[i] != te_ref[jnp.maximum(i - 1, 0)])

    @pl.when(changed)
    def _():
        wbf_ref[...] = w_ref[...].astype(BF16)

    @pl.when(tv_ref[i] == 1)
    def _():
        o_ref[...] = jnp.dot(u_ref[...], wbf_ref[...], preferred_element_type=F32)

    @pl.when(tv_ref[i] == 0)
    def _():
        o_ref[...] = jnp.zeros(o_ref.shape, o_ref.dtype)


def moe_down(u, w2, lm, te, tv, dm):
    p, f = u.shape
    d = w2.shape[-1]
    tg, tn = dm.tg, dm.tn
    grid_spec = pltpu.PrefetchScalarGridSpec(
        num_scalar_prefetch=2,
        grid=(d // tn, p // tg),
        in_specs=[pl.BlockSpec((tg, f), lambda j, i, te, tv: (i, 0)),
                  pl.BlockSpec((None, None, f, tn), lambda j, i, te, tv: (lm, te[i], 0, j))],
        out_specs=pl.BlockSpec((tg, tn), lambda j, i, te, tv: (i, j)),
        scratch_shapes=[pltpu.VMEM((f, tn), BF16)])
    return pl.pallas_call(
        _gdown_kernel,
        grid_spec=grid_spec,
        out_shape=jax.ShapeDtypeStruct((p, d), F32),
        compiler_params=_cp("arbitrary", "arbitrary"),
        name="moe_down",
    )(te, tv, u, w2)


def _router_kernel(a_ref, w_ref, o_ref, *, n_experts):
    logits = jnp.dot(a_ref[...].astype(BF16), w_ref[...].astype(BF16),
                     preferred_element_type=F32)
    lane = lax.broadcasted_iota(jnp.int32, logits.shape, 1)
    lg = jnp.where(lane < n_experts, logits, NEG)
    v1 = jnp.max(lg, axis=1, keepdims=True)
    i1 = jnp.min(jnp.where(lg == v1, lane, LANES), axis=1, keepdims=True)
    lg2 = jnp.where(lane == i1, NEG, lg)
    v2 = jnp.max(lg2, axis=1, keepdims=True)
    i2 = jnp.min(jnp.where(lg2 == v2, lane, LANES), axis=1, keepdims=True)
    e2 = jnp.exp(v2 - v1)
    g1 = 1.0 / (1.0 + e2)
    g2 = e2 / (1.0 + e2)
    out = jnp.where(lane == 0, g1, 0.0)
    out = jnp.where(lane == 1, g2, out)
    out = jnp.where(lane == 2, i1.astype(F32), out)
    out = jnp.where(lane == 3, i2.astype(F32), out)
    o_ref[...] = out


def router(hf, router_w, lm, dm):
    m, k = hf.shape
    e = dm.n_experts
    tm = dm.tm
    wpad = jnp.zeros((k, LANES), F32).at[:, :e].set(router_w[lm])
    return pl.pallas_call(
        functools.partial(_router_kernel, n_experts=e),
        grid=(m // tm,),
        in_specs=[pl.BlockSpec((tm, k), lambda i: (i, 0)),
                  pl.BlockSpec((k, LANES), lambda i: (0, 0))],
        out_specs=pl.BlockSpec((tm, LANES), lambda i: (i, 0)),
        out_shape=jax.ShapeDtypeStruct((m, LANES), F32),
        compiler_params=_cp("parallel"),
        name="router",
    )(hf, wpad)


def _row_copy(src_ref, dst_ref, src_row, dst_row, sem):
    return pltpu.make_async_copy(src_ref.at[pl.ds(src_row, 1), :],
                                 dst_ref.at[pl.ds(dst_row, 1), :], sem)


def _gather_kernel(idx_ref, src_ref, o_ref, buf_ref, sem, *, rows):
    def issue(r, carry):
        _row_copy(src_ref, buf_ref, idx_ref[0, 0, r], r, sem).start()
        return carry

    lax.fori_loop(0, rows, issue, 0, unroll=DMA_UNROLL)

    def wait(r, carry):
        _row_copy(src_ref, buf_ref, 0, r, sem).wait()
        return carry

    lax.fori_loop(0, rows, wait, 0, unroll=DMA_UNROLL)
    o_ref[...] = buf_ref[...].astype(o_ref.dtype)


def gather_rows(src, idx, rows, out_dtype):
    p = idx.shape[0]
    d = src.shape[1]
    nblk = p // rows
    return pl.pallas_call(
        functools.partial(_gather_kernel, rows=rows),
        grid=(nblk,),
        in_specs=[pl.BlockSpec((1, 1, rows), lambda i: (i, 0, 0), memory_space=pltpu.SMEM),
                  pl.BlockSpec(memory_space=pl.ANY)],
        out_specs=pl.BlockSpec((rows, d), lambda i: (i, 0)),
        out_shape=jax.ShapeDtypeStruct((p, d), out_dtype),
        scratch_shapes=[pltpu.VMEM((rows, d), src.dtype), pltpu.SemaphoreType.DMA(())],
        compiler_params=_cp("arbitrary"),
        name="gather_rows",
    )(idx.reshape(nblk, 1, rows), src)


def _combine_kernel(pos_ref, ys_ref, x_ref, r_ref, gate_ref, o_ref, buf_ref, sem, *, rows):
    def issue(r, carry):
        _row_copy(ys_ref, buf_ref.at[0], pos_ref[0, 0, 2 * r], r, sem).start()
        _row_copy(ys_ref, buf_ref.at[1], pos_ref[0, 0, 2 * r + 1], r, sem).start()
        return carry

    lax.fori_loop(0, rows, issue, 0, unroll=DMA_UNROLL)

    def wait(r, carry):
        _row_copy(ys_ref, buf_ref.at[0], 0, r, sem).wait()
        _row_copy(ys_ref, buf_ref.at[1], 0, r, sem).wait()
        return carry

    lax.fori_loop(0, rows, wait, 0, unroll=DMA_UNROLL)
    rr = r_ref[...]
    mix = rr[:, 0:1] * buf_ref[0] + rr[:, 1:2] * buf_ref[1]
    o_ref[...] = x_ref[...] + gate_ref[...] * mix


def moe_combine(ys, pos, x, rinfo, mod3, gate_idx, dm):
    m, d = x.shape
    rows = dm.tc
    nblk = m // rows
    tpb = dm.seq // rows
    return pl.pallas_call(
        functools.partial(_combine_kernel, rows=rows),
        grid=(nblk,),
        in_specs=[pl.BlockSpec((1, 1, 2 * rows), lambda i: (i, 0, 0), memory_space=pltpu.SMEM),
                  pl.BlockSpec(memory_space=pl.ANY),
                  pl.BlockSpec((rows, d), lambda i: (i, 0)),
                  pl.BlockSpec((rows, LANES), lambda i: (i, 0)),
                  pl.BlockSpec((None, 1, d), lambda i: (i // tpb, 0, gate_idx))],
        out_specs=pl.BlockSpec((rows, d), lambda i: (i, 0)),
        out_shape=jax.ShapeDtypeStruct((m, d), F32),
        scratch_shapes=[pltpu.VMEM((2, rows, d), F32), pltpu.SemaphoreType.DMA(())],
        compiler_params=_cp("arbitrary"),
        name="moe_combine",
    )(pos.reshape(nblk, 1, 2 * rows), ys, x, rinfo, mod3)


def routing_tables(rinfo, dm):
    m = rinfo.shape[0]
    e, tg = dm.n_experts, dm.tg
    n_tiles = (TOP_K * m) // tg + e
    eid = rinfo[:, 2:2 + TOP_K].astype(jnp.int32).reshape(-1)
    onehot = (eid[:, None] == jnp.arange(e, dtype=jnp.int32)[None, :]).astype(jnp.int32)
    csum = jnp.cumsum(onehot, axis=0)
    rank = jnp.sum((csum - onehot) * onehot, axis=1)
    counts = csum[-1]
    tiles_per = (counts + tg - 1) // tg
    tile_end = jnp.cumsum(tiles_per)
    tile_start = tile_end - tiles_per
    pos = tile_start[eid] * tg + rank
    token = jnp.arange(TOP_K * m, dtype=jnp.int32) // TOP_K
    slot_token = jnp.zeros((n_tiles * tg,), jnp.int32).at[pos].set(token)
    tidx = jnp.arange(n_tiles, dtype=jnp.int32)
    used = tile_end[-1]
    te_raw = jnp.sum((tidx[:, None] >= tile_end[None, :]).astype(jnp.int32), axis=1)
    last_e = jnp.sum((jnp.maximum(used - 1, 0) >= tile_end).astype(jnp.int32))
    te = jnp.where(tidx < used, te_raw, last_e).astype(jnp.int32)
    tv = (tidx < used).astype(jnp.int32)
    return pos.astype(jnp.int32), slot_token, te, tv


def forward(dm, x, c, rel_bias, w_mod, b_mod, norm_mix_g, norm_ff_g, w_in,
            da_q_g, da_k_g, da_lam_q1, da_lam_k1, da_lam_q2, da_lam_k2, da_sub_g,
            ml_conv_w, ml_conv_b, ml_i_bias, ml_f_bias, ml_norm_g,
            w_branch_a, w_branch_b, w_out,
            ffn_w1, ffn_w3, ffn_w2, router_w, moe_w1, moe_w3, moe_w2):
    b, t, d = x.shape
    m = b * t
    h = dm.da_heads
    xf = x.astype(F32).reshape(m, d)
    mod_all = adaln_mod(c, w_mod, b_mod)
    d0, e0 = bias_tiles(rel_bias)
    c_gate = dm.n_main + 2 * dm.ml_heads
    w_g = w_in[:, :, c_gate:]

    for l in range(dm.depth):
        mod3 = mod_all[l].reshape(b, 1, 6 * d)
        hb = norm_mod(xf, norm_mix_g[l], mod3, 1, 0, dm, BF16)
        proj = mm_act(hb, w_in, l, dm.n_main, None, BF16, dm, "proj_main")
        sg = mm_act(hb, w_g, l, 2 * d, "sigmoid", BF16, dm, "branch_gates")
        g, gt = if_gates(hb, w_in, l, ml_i_bias[l], ml_f_bias[l], dm)
        lam_init = 0.8 - 0.6 * math.exp(-0.3 * l)
        lamv = jnp.stack([da_lam_q1[l], da_lam_k1[l], da_lam_q2[l], da_lam_k2[l]]).astype(F32)
        ya = diff_attention(proj, d0, e0, lamv, da_sub_g[l].astype(F32), da_q_g[l], da_k_g[l],
                            lam_init, dm)
        yb = mlstm_branch(proj, g, gt, ml_conv_w[l], ml_conv_b[l], ml_norm_g[l].astype(F32), dm)
        merged = merge_branches(ya, yb, w_branch_a, w_branch_b, l, sg, dm)
        xf = mm_residual(merged, w_out, l, xf, mod3, 2, dm)
        if l % 2 == 0:
            hb = norm_mod(xf, norm_ff_g[l], mod3, 4, 3, dm, BF16)
            u = ffn_up(hb, ffn_w1, ffn_w3, l // 2, dm)
            xf = mm_residual(u, ffn_w2, l // 2, xf, mod3, 5, dm)
        else:
            lm = l // 2
            hf = norm_mod(xf, norm_ff_g[l], mod3, 4, 3, dm, F32)
            rinfo = router(hf, router_w, lm, dm)
            pos, slot_token, te, tv = routing_tables(rinfo, dm)
            hs = gather_rows(hf, slot_token, dm.tg, BF16)
            u = moe_up(hs, moe_w1, moe_w3, lm, te, tv, dm)
            ys = moe_down(u, moe_w2, lm, te, tv, dm)
            xf = moe_combine(ys, pos, xf, rinfo, mod3, 5, dm)
    return xf.reshape(b, t, d)


def kernel(x, c, rel_bias, w_mod, b_mod, norm_mix_g, norm_ff_g, w_in, da_q_g, da_k_g, da_lam_q1, da_lam_k1, da_lam_q2, da_lam_k2, da_sub_g, ml_conv_w, ml_conv_b, ml_i_bias, ml_f_bias, ml_norm_g, w_branch_a, w_branch_b, w_out, ffn_w1, ffn_w3, ffn_w2, router_w, moe_w1, moe_w3, moe_w2):
    b, t, d = x.shape
    dm = Dims(d_model=d, batch=b, seq=t, depth=w_mod.shape[0], d_ff=ffn_w1.shape[-1],
              n_experts=router_w.shape[-1])
    return forward(dm, x, c, rel_bias, w_mod, b_mod, norm_mix_g, norm_ff_g, w_in,
                   da_q_g, da_k_g, da_lam_q1, da_lam_k1, da_lam_q2, da_lam_k2, da_sub_g,
                   ml_conv_w, ml_conv_b, ml_i_bias, ml_f_bias, ml_norm_g,
                   w_branch_a, w_branch_b, w_out,
                   ffn_w1, ffn_w3, ffn_w2, router_w, moe_w1, moe_w3, moe_w2)
```

```python
import functools
import math
from typing import NamedTuple

import numpy as np
import jax
import jax.numpy as jnp
from jax import lax
from jax.experimental import pallas as pl
from jax.experimental.pallas import tpu as pltpu

F32 = jnp.float32
BF16 = jnp.bfloat16

DA_DQK = 64
DA_DV = 128
ML_DQK = 128
ML_DV = 256
CONV_W = 4
TOP_K = 2
REL_BUCKETS = 32
REL_MAX_DIST = 128
EPS = 1e-6
NEG = -1e30
LOG2E = math.log2(math.e)
LANES = 128
SUBLANES = 8
VMEM_LIMIT = 52 * 1024 * 1024
DMA_UNROLL = 8


class Dims(NamedTuple):
    d_model: int
    batch: int
    seq: int
    depth: int
    d_ff: int
    n_experts: int
    tm: int = 512
    tm_dense: int = 1024
    tn: int = 512
    tn_wide: int = 1024
    tq: int = 1024
    ml_chunk: int = 256
    tg: int = 1024
    tg_down: int = 512
    tc: int = 256
    attn_rc: int = 2048

    @property
    def d_branch(self):
        return self.d_model // 2

    @property
    def da_heads(self):
        return self.d_branch // DA_DV

    @property
    def ml_heads(self):
        return self.d_branch // ML_DV

    @property
    def n_main(self):
        return 3 * self.d_branch + 2 * self.ml_heads * ML_DQK + 2 * self.d_branch

    @property
    def tokens(self):
        return self.batch * self.seq


def _cp(*sem):
    return pltpu.CompilerParams(dimension_semantics=sem, vmem_limit_bytes=VMEM_LIMIT)


def _sigmoid(x):
    return 1.0 / (1.0 + jnp.exp(-x))


def _log_sigmoid(x):
    return jnp.minimum(x, 0.0) - jnp.log(1.0 + jnp.exp(-jnp.abs(x)))


def _mod_kernel(c_ref, w_ref, b_ref, o_ref):
    c = c_ref[...]
    ca = c * _sigmoid(c)
    o_ref[...] = jnp.dot(ca, w_ref[...], preferred_element_type=F32) + b_ref[...]


def adaln_mod(c, w_mod, b_mod):
    depth, d, n = w_mod.shape
    b = c.shape[0]
    bp = -(-b // SUBLANES) * SUBLANES
    cp = jnp.zeros((bp, d), F32).at[:b].set(c.astype(F32))
    tn = min(n, 1024)
    out = pl.pallas_call(
        _mod_kernel,
        grid=(depth, n // tn),
        in_specs=[pl.BlockSpec((bp, d), lambda l, j: (0, 0)),
                  pl.BlockSpec((None, d, tn), lambda l, j: (l, 0, j)),
                  pl.BlockSpec((None, 1, tn), lambda l, j: (l, 0, j))],
        out_specs=pl.BlockSpec((None, bp, tn), lambda l, j: (l, 0, j)),
        out_shape=jax.ShapeDtypeStruct((depth, bp, n), F32),
        compiler_params=_cp("parallel", "parallel"),
        name="adaln_mod",
    )(cp, w_mod, b_mod.reshape(depth, 1, n))
    return out[:, :b]


def _norm_mod_kernel(x_ref, g_ref, sc_ref, sh_ref, o_ref):
    x = x_ref[...]
    ms = jnp.mean(x * x, axis=-1, keepdims=True)
    h = x * lax.rsqrt(ms + EPS) * g_ref[...] * (1.0 + sc_ref[...]) + sh_ref[...]
    o_ref[...] = h.astype(o_ref.dtype)


def norm_mod(x, g, mod3, scale_idx, shift_idx, dm, out_dtype):
    m, d = x.shape
    tm = dm.tm_dense
    tpb = dm.seq // tm
    return pl.pallas_call(
        _norm_mod_kernel,
        grid=(m // tm,),
        in_specs=[pl.BlockSpec((tm, d), lambda i: (i, 0)),
                  pl.BlockSpec((1, d), lambda i: (0, 0)),
                  pl.BlockSpec((None, 1, d), lambda i: (i // tpb, 0, scale_idx)),
                  pl.BlockSpec((None, 1, d), lambda i: (i // tpb, 0, shift_idx))],
        out_specs=pl.BlockSpec((tm, d), lambda i: (i, 0)),
        out_shape=jax.ShapeDtypeStruct((m, d), out_dtype),
        compiler_params=_cp("parallel"),
        name="norm_mod",
    )(x, g.reshape(1, d), mod3, mod3)


def _mm_act_kernel(a_ref, w_ref, o_ref, wbf_ref, *, act):
    @pl.when(pl.program_id(1) == 0)
    def _():
        wbf_ref[...] = w_ref[...].astype(BF16)

    acc = jnp.dot(a_ref[...], wbf_ref[...], preferred_element_type=F32)
    if act == "sigmoid":
        acc = _sigmoid(acc)
    o_ref[...] = acc.astype(o_ref.dtype)


def mm_act(a, w3, layer, n, act, out_dtype, dm, name):
    m, k = a.shape
    tm, tn = dm.tm_dense, dm.tn_wide
    return pl.pallas_call(
        functools.partial(_mm_act_kernel, act=act),
        grid=(n // tn, m // tm),
        in_specs=[pl.BlockSpec((tm, k), lambda j, i: (i, 0)),
                  pl.BlockSpec((None, k, tn), lambda j, i: (layer, 0, j))],
        out_specs=pl.BlockSpec((tm, tn), lambda j, i: (i, j)),
        out_shape=jax.ShapeDtypeStruct((m, n), out_dtype),
        scratch_shapes=[pltpu.VMEM((k, tn), BF16)],
        compiler_params=_cp("arbitrary", "arbitrary"),
        name=name,
    )(a, w3)


def _ifgate_kernel(a_ref, w_ref, wt_ref, bc_ref, br_ref, g_ref, gt_ref, *, hm):
    a = a_ref[...]
    gc = jnp.dot(a, w_ref[...].astype(BF16), preferred_element_type=F32) + bc_ref[...]
    lane = lax.broadcasted_iota(jnp.int32, gc.shape, 1)
    g_ref[...] = jnp.where((lane >= hm) & (lane < 2 * hm), _log_sigmoid(gc), gc)
    gt = lax.dot_general(wt_ref[...].astype(BF16), a, (((1,), (1,)), ((), ())),
                         preferred_element_type=F32) + br_ref[...]
    row = lax.broadcasted_iota(jnp.int32, gt.shape, 0)
    gt_ref[...] = jnp.where((row >= hm) & (row < 2 * hm), _log_sigmoid(gt), gt)


def if_gates(h, w_in, layer, i_bias, f_bias, dm):
    m, k = h.shape
    hm = dm.ml_heads
    tm = dm.tm
    tpb = dm.seq // tm
    c0 = dm.n_main
    wt = jnp.zeros((SUBLANES, k), F32).at[:2 * hm].set(w_in[layer, :, c0:c0 + 2 * hm].T)
    bias = jnp.concatenate([i_bias, f_bias]).astype(F32)
    bc = jnp.zeros((1, LANES), F32).at[0, :2 * hm].set(bias)
    br = jnp.zeros((SUBLANES, 1), F32).at[:2 * hm, 0].set(bias)
    return pl.pallas_call(
        functools.partial(_ifgate_kernel, hm=hm),
        grid=(m // tm,),
        in_specs=[pl.BlockSpec((tm, k), lambda i: (i, 0)),
                  pl.BlockSpec((None, k, LANES), lambda i: (layer, 0, c0 // LANES)),
                  pl.BlockSpec((SUBLANES, k), lambda i: (0, 0)),
                  pl.BlockSpec((1, LANES), lambda i: (0, 0)),
                  pl.BlockSpec((SUBLANES, 1), lambda i: (0, 0))],
        out_specs=[pl.BlockSpec((tm, LANES), lambda i: (i, 0)),
                   pl.BlockSpec((None, SUBLANES, tm), lambda i: (i // tpb, 0, i % tpb))],
        out_shape=[jax.ShapeDtypeStruct((m, LANES), F32),
                   jax.ShapeDtypeStruct((dm.batch, SUBLANES, dm.seq), F32)],
        compiler_params=_cp("parallel"),
        name="if_gates",
    )(h, w_in, wt, bc, br)


def _t5_bucket_np(rel):
    n = jnp.maximum(rel, 0)
    max_exact = REL_BUCKETS // 2
    large = max_exact + (jnp.log(jnp.maximum(n, 1).astype(F32) / max_exact)
                         / math.log(REL_MAX_DIST / max_exact)
                         * (REL_BUCKETS - max_exact)).astype(jnp.int32)
    large = jnp.minimum(large, REL_BUCKETS - 1)
    return jnp.where(n < max_exact, n, large)


def _bias_tile_kernel(tab_ref, idd_ref, ide_ref, d_ref, e_ref):
    h2 = pl.program_id(0)
    far = tab_ref[REL_BUCKETS - 1, h2]
    idd = idd_ref[...]
    ide = ide_ref[...]
    d = jnp.zeros(idd.shape, F32)
    e = jnp.zeros(ide.shape, F32)
    for b in range(REL_BUCKETS):
        t = tab_ref[b, h2] - far
        d = jnp.where(idd == b, t, d)
        e = jnp.where(ide == b, t, e)
    r = lax.broadcasted_iota(jnp.int32, idd.shape, 0)
    c = lax.broadcasted_iota(jnp.int32, idd.shape, 1)
    d_ref[...] = jnp.where(r <= c, d * LOG2E, NEG)
    e_ref[...] = e * LOG2E


def bias_tiles(rel_bias):
    nb, h2 = rel_bias.shape
    r = jnp.arange(LANES, dtype=jnp.int32)[:, None]
    c = jnp.arange(LANES, dtype=jnp.int32)[None, :]
    idd = _t5_bucket_np(c - r)
    ide = _t5_bucket_np(LANES + c - r)
    spec = pl.BlockSpec((LANES, LANES), lambda h: (0, 0))
    ospec = pl.BlockSpec((None, LANES, LANES), lambda h: (h, 0, 0))
    return pl.pallas_call(
        _bias_tile_kernel,
        grid=(h2,),
        in_specs=[pl.BlockSpec(memory_space=pltpu.SMEM), spec, spec],
        out_specs=[ospec, ospec],
        out_shape=[jax.ShapeDtypeStruct((h2, LANES, LANES), F32)] * 2,
        compiler_params=_cp("arbitrary"),
        name="bias_tiles",
    )(rel_bias.astype(F32), idd, ide)


V_ROWS = DA_DV + 16
BOUND_MARGIN = 1.02
MAX_BOUND_SPREAD = 100.0


def _attn_kernel(q_ref, k_ref, v_ref, d0_ref, e0_ref, lam_ref, subg_ref, gq_ref, gk_ref, o_ref,
                 qq_ref, kn_ref, vt_ref, m_ref, acc_ref, bd_ref, be_ref, stat_ref,
                 *, tq, rc, lam_init):
    qi = pl.program_id(2)
    nsub = tq // LANES

    @pl.when(qi == 0)
    def _():
        for mp in range(2):
            for a in range(nsub):
                for b in range(nsub):
                    rows = pl.ds(b * LANES, LANES)
                    cols = pl.ds(mp * tq + a * LANES, LANES)
                    if a == b:
                        bd_ref[rows, cols] = d0_ref[mp]
                    elif a == b + 1:
                        bd_ref[rows, cols] = e0_ref[mp]
                    elif a < b:
                        bd_ref[rows, cols] = jnp.full((LANES, LANES), NEG, F32)
                    else:
                        bd_ref[rows, cols] = jnp.zeros((LANES, LANES), F32)
                    if a == 0 and b == nsub - 1:
                        be_ref[rows, cols] = e0_ref[mp]
                    else:
                        be_ref[rows, cols] = jnp.zeros((LANES, LANES), F32)

        def prep_kv(ci, carry):
            rows = pl.ds(pl.multiple_of(ci * tq, tq), tq)
            kf = k_ref[rows, :].astype(F32)
            low = lax.broadcasted_iota(jnp.int32, kf.shape, 1) < DA_DQK
            sq = kf * kf
            ss1 = jnp.sum(jnp.where(low, sq, 0.0), axis=1, keepdims=True)
            ss2 = jnp.sum(jnp.where(low, 0.0, sq), axis=1, keepdims=True)
            rs = lax.rsqrt(jnp.where(low, ss1, ss2) * (1.0 / DA_DQK) + EPS)
            kn = (kf * rs * gk_ref[...]).astype(BF16)
            kn_ref[rows, :] = kn
            kq = kn.astype(F32) * kn.astype(F32)
            n1 = jnp.sum(jnp.where(low, kq, 0.0), axis=1, keepdims=True)
            n2 = jnp.sum(jnp.where(low, 0.0, kq), axis=1, keepdims=True)
            vt_ref[ci, pl.ds(0, DA_DV), :] = v_ref[rows, :].astype(F32).T.astype(BF16)
            pad_row = lax.broadcasted_iota(jnp.int32, (V_ROWS - DA_DV, tq), 0)
            vt_ref[ci, pl.ds(DA_DV, V_ROWS - DA_DV), :] = jnp.where(pad_row == 0, 1.0, 0.0).astype(BF16)
            return jnp.maximum(carry, jnp.max(jnp.maximum(n1, n2), axis=0, keepdims=True))

        ksq_max = lax.fori_loop(0, k_ref.shape[0] // tq, prep_kv, jnp.zeros((1, 1), F32))
        d0v = d0_ref[...]
        e0v = e0_ref[...]
        valid = d0v > 0.5 * NEG
        b_hi = jnp.maximum(jnp.maximum(jnp.max(jnp.where(valid, d0v, 0.0)), jnp.max(e0v)), 0.0)
        b_lo = jnp.minimum(jnp.minimum(jnp.min(jnp.where(valid, d0v, 0.0)), jnp.min(e0v)), 0.0)
        stat_ref[0:1, :] = jnp.broadcast_to(jnp.sqrt(ksq_max), (1, LANES))
        stat_ref[1:2, :] = jnp.full((1, LANES), b_hi, F32)
        stat_ref[2:3, :] = jnp.full((1, LANES), b_hi - b_lo, F32)

    qf = q_ref[...].astype(F32)
    lane = lax.broadcasted_iota(jnp.int32, qf.shape, 1)
    for mp in range(2):
        qm = jnp.where((lane < DA_DQK) == (mp == 0), qf, 0.0)
        ss = jnp.sum(qm * qm, axis=1, keepdims=True)
        qn = qm * lax.rsqrt(ss * (1.0 / DA_DQK) + EPS) * gq_ref[...]
        qq_ref[pl.ds(mp * tq, tq), :] = qn.astype(BF16)
    acc_ref[...] = jnp.zeros(acc_ref.shape, F32)

    nchunk = 2 * tq // rc

    qq = qq_ref[...].astype(F32)
    qn2 = lax.dot_general(jnp.ones((SUBLANES, LANES), BF16), (qq * qq).astype(BF16),
                          (((1,), (1,)), ((), ())), preferred_element_type=F32)[0:1]
    m_bound = jnp.sqrt(qn2) * (BOUND_MARGIN * stat_ref[0:1, 0:1]) + stat_ref[1:2, 0:1]
    spread = 2.0 * jnp.max(m_bound, axis=1, keepdims=True) + stat_ref[2:3, 0:1]
    bounded = spread[0, 0] <= MAX_BOUND_SPREAD

    all_cols = [(c * rc, rc) for c in range(nchunk)]
    half = tq // 2

    def update(ki, bias_ref, online, k0=0, klen=tq, col_ranges=all_cols):
        k = kn_ref[pl.ds(pl.multiple_of(ki * tq + k0, half), klen), :]
        vt = vt_ref[ki][:, k0:k0 + klen]
        ss = [lax.dot_general(k, qq_ref[pl.ds(c0, cl), :], (((1,), (1,)), ((), ())),
                              preferred_element_type=F32) for c0, cl in col_ranges]
        m_prev_all = m_ref[...]
        results = []
        for s, (c0, cl) in zip(ss, col_ranges):
            if bias_ref is not None:
                s = s + bias_ref[pl.ds(k0, klen), pl.ds(c0, cl)]
            m_prev = m_prev_all[:, c0:c0 + cl]
            m_new = jnp.maximum(m_prev, jnp.max(s, axis=0, keepdims=True)) if online else m_prev
            p = jnp.exp2(s - m_new).astype(BF16)
            pv = jnp.dot(vt, p, preferred_element_type=F32)
            results.append((m_new, jnp.exp2(m_prev - m_new) if online else None, pv))
        for (m_new, alpha, pv), (c0, cl) in zip(results, col_ranges):
            cols = pl.ds(c0, cl)
            if online:
                acc_ref[:, cols] = alpha * acc_ref[:, cols] + pv
                m_ref[:, cols] = m_new
            else:
                acc_ref[:, cols] = acc_ref[:, cols] + pv

    def attend(online):
        def far_body(ki, carry):
            update(ki, None, online)
            return carry

        lax.fori_loop(0, jnp.maximum(qi - 1, 0), far_body, 0)

        @pl.when(qi >= 1)
        def _():
            update(qi - 1, be_ref, online)

        update(qi, bd_ref, online, 0, half)
        update(qi, bd_ref, online, half, half, [(half, half), (tq + half, half)])

    @pl.when(bounded)
    def _():
        m_ref[...] = m_bound
        attend(False)

    @pl.when(jnp.logical_not(bounded))
    def _():
        m_ref[...] = jnp.full(m_ref.shape, NEG, F32)
        attend(True)

    acc = acc_ref[...]
    o = acc[:DA_DV] / acc[DA_DV:DA_DV + 1]
    lamv = lam_ref[...]
    lam = (jnp.exp(jnp.sum(lamv[0:1] * lamv[1:2], axis=1, keepdims=True))
           - jnp.exp(jnp.sum(lamv[2:3] * lamv[3:4], axis=1, keepdims=True)) + lam_init)
    od = o[:, :tq] - lam * o[:, tq:]
    ms = jnp.mean(od * od, axis=0, keepdims=True)
    y = od * lax.rsqrt(ms + EPS) * subg_ref[...] * (1.0 - lam_init)
    o_ref[...] = y.T.astype(o_ref.dtype)


def diff_attention(proj, d0, e0, lamv, sub_g, q_g, k_g, lam_init, dm):
    m = proj.shape[0]
    t, tq, h = dm.seq, dm.tq, dm.da_heads
    nq = t // tq
    gq = (jnp.tile(q_g.astype(F32), 2) * (DA_DQK ** -0.5 * LOG2E)).reshape(1, LANES)
    gk = jnp.tile(k_g.astype(F32), 2).reshape(1, LANES)
    vec = pl.BlockSpec((1, LANES), lambda b, hh, qi: (0, 0))
    return pl.pallas_call(
        functools.partial(_attn_kernel, tq=tq, rc=min(dm.attn_rc, 2 * tq), lam_init=lam_init),
        grid=(dm.batch, h, nq),
        in_specs=[pl.BlockSpec((tq, LANES), lambda b, hh, qi: (b * nq + qi, hh)),
                  pl.BlockSpec((t, LANES), lambda b, hh, qi: (b, h + hh)),
                  pl.BlockSpec((t, LANES), lambda b, hh, qi: (b, 2 * h + hh)),
                  pl.BlockSpec((2, LANES, LANES), lambda b, hh, qi: (hh, 0, 0)),
                  pl.BlockSpec((2, LANES, LANES), lambda b, hh, qi: (hh, 0, 0)),
                  pl.BlockSpec((4, DA_DQK), lambda b, hh, qi: (0, 0)),
                  pl.BlockSpec((DA_DV, 1), lambda b, hh, qi: (0, 0)),
                  vec, vec],
        out_specs=pl.BlockSpec((tq, LANES), lambda b, hh, qi: (b * nq + qi, hh)),
        out_shape=jax.ShapeDtypeStruct((m, dm.d_branch), BF16),
        scratch_shapes=[pltpu.VMEM((2 * tq, LANES), BF16),
                        pltpu.VMEM((t, LANES), BF16),
                        pltpu.VMEM((nq, V_ROWS, tq), BF16),
                        pltpu.VMEM((1, 2 * tq), F32),
                        pltpu.VMEM((V_ROWS, 2 * tq), F32),
                        pltpu.VMEM((tq, 2 * tq), F32),
                        pltpu.VMEM((tq, 2 * tq), F32),
                        pltpu.VMEM((SUBLANES, LANES), F32)],
        compiler_params=_cp("parallel", "parallel", "arbitrary"),
        name="diff_attention",
    )(proj, proj, proj, d0, e0, lamv, sub_g.reshape(DA_DV, 1), gq, gk)


def _mlstm_kernel(q_ref, k_ref, v_ref, og_ref, g_ref, gt_ref, cwq_ref, cwk_ref,
                  cbq_ref, cbk_ref, ng_ref, o_ref,
                  qbuf_ref, kbuf_ref, ct_ref, n_ref, m_ref, *, L, hm):
    c = pl.program_id(1)

    @pl.when(c == 0)
    def _():
        qbuf_ref[pl.ds(0, SUBLANES), :] = jnp.zeros((SUBLANES, hm * ML_DQK), F32)
        kbuf_ref[pl.ds(0, SUBLANES), :] = jnp.zeros((SUBLANES, hm * ML_DQK), F32)
        ct_ref[...] = jnp.zeros(ct_ref.shape, F32)
        n_ref[...] = jnp.zeros(n_ref.shape, F32)
        m_ref[...] = jnp.zeros(m_ref.shape, F32)

    def conv_silu(x_ref, buf_ref, w_ref, b_ref):
        buf_ref[pl.ds(SUBLANES, L), :] = x_ref[...].astype(F32)
        w = w_ref[...]
        y = b_ref[...] + w[CONV_W - 1:CONV_W] * buf_ref[pl.ds(SUBLANES, L), :]
        for s in range(1, CONV_W):
            y = y + w[CONV_W - 1 - s:CONV_W - s] * buf_ref[pl.ds(SUBLANES - s, L), :]
        tail = buf_ref[pl.ds(L, SUBLANES), :]
        buf_ref[pl.ds(0, SUBLANES), :] = tail
        return y * _sigmoid(y)

    q_all = conv_silu(q_ref, qbuf_ref, cwq_ref, cbq_ref)
    k_all = conv_silu(k_ref, kbuf_ref, cwk_ref, cbk_ref) * (ML_DQK ** -0.5)
    v_all = v_ref[...]
    og_all = og_ref[...]
    ng_all = ng_ref[...]

    g = g_ref[...]
    gt = gt_ref[...]
    ii = lax.broadcasted_iota(jnp.int32, (L, L), 0)
    jj = lax.broadcasted_iota(jnp.int32, (L, L), 1)
    tril = (jj <= ii).astype(F32)
    triu = (ii <= jj).astype(F32)
    bcum = jnp.dot(tril, g, preferred_element_type=F32, precision=lax.Precision.HIGHEST)
    btcum = jnp.dot(gt, triu, preferred_element_type=F32, precision=lax.Precision.HIGHEST)
    lane = lax.broadcasted_iota(jnp.int32, g.shape, 1)
    row = lax.broadcasted_iota(jnp.int32, gt.shape, 0)
    ct_all = ct_ref[...]
    n_all = n_ref[...]
    m_all = m_ref[...]

    new_state = []
    for h in range(hm):
        q = q_all[:, h * ML_DQK:(h + 1) * ML_DQK]
        k = k_all[:, h * ML_DQK:(h + 1) * ML_DQK]
        v = v_all[:, h * ML_DV:(h + 1) * ML_DV]
        ig_col = jnp.sum(jnp.where(lane == h, g, 0.0), axis=1, keepdims=True)
        b_col = jnp.sum(jnp.where(lane == hm + h, bcum, 0.0), axis=1, keepdims=True)
        ig_row = jnp.sum(jnp.where(row == h, gt, 0.0), axis=0, keepdims=True)
        b_row = jnp.sum(jnp.where(row == hm + h, btcum, 0.0), axis=0, keepdims=True)

        m_old = m_all[h]
        dlog = jnp.where(jj <= ii, b_col - b_row + ig_row, NEG)
        m_inter = b_col + m_old
        m_s = jnp.maximum(m_inter, jnp.max(dlog, axis=1, keepdims=True))
        dmat = jnp.exp(dlog - m_s)
        qb = q.astype(BF16)
        kb = k.astype(BF16)
        qk = lax.dot_general(qb, kb, (((1,), (1,)), ((), ())), preferred_element_type=F32)
        w = dmat * qk
        inter = jnp.exp(m_inter - m_s)
        ct = ct_all[h]
        n_row = n_all[h]
        num = (jnp.dot(w.astype(BF16), v, preferred_element_type=F32)
               + inter * jnp.dot(qb, ct.astype(BF16), preferred_element_type=F32))
        den = (jnp.sum(w, axis=1, keepdims=True)
               + inter * jnp.sum(q * n_row, axis=1, keepdims=True))
        hh = num / jnp.maximum(jnp.abs(den), jnp.exp(-m_s))

        m_new = m_s[L - 1:L]
        b_last = b_col[L - 1:L]
        w_end = jnp.exp(b_last - b_col + ig_col - m_new)
        decay = jnp.exp(b_last + m_old - m_new)
        kw = k * w_end
        ct_new = decay * ct + lax.dot_general(kw.astype(BF16), v, (((0,), (0,)), ((), ())),
                                              preferred_element_type=F32)
        n_new = decay * n_row + jnp.sum(kw, axis=0, keepdims=True)

        ms = jnp.mean(hh * hh, axis=-1, keepdims=True)
        cols = slice(h * ML_DV, (h + 1) * ML_DV)
        y = hh * lax.rsqrt(ms + EPS) * ng_all[:, cols] * _sigmoid(og_all[:, cols].astype(F32))
        new_state.append((ct_new, n_new, m_new, y))

    for h, (ct_new, n_new, m_new, y) in enumerate(new_state):
        ct_ref[h] = ct_new
        n_ref[h] = n_new
        m_ref[h] = m_new
        o_ref[:, pl.ds(h * ML_DV, ML_DV)] = y.astype(o_ref.dtype)


def mlstm_branch(proj, g, gt, conv_w, conv_b, norm_g, dm):
    m = proj.shape[0]
    t, L, hm, h = dm.seq, dm.ml_chunk, dm.ml_heads, dm.da_heads
    nc = t // L
    wq, wv = hm * ML_DQK, hm * ML_DV
    qc, kc = 3 * dm.d_branch // wq, 3 * dm.d_branch // wq + 1
    vc, oc = (3 * dm.d_branch + 2 * wq) // wv, (3 * dm.d_branch + 2 * wq) // wv + 1
    cw = conv_w.astype(F32)
    cb = conv_b.astype(F32).reshape(1, -1)
    return pl.pallas_call(
        functools.partial(_mlstm_kernel, L=L, hm=hm),
        grid=(dm.batch, nc),
        in_specs=[pl.BlockSpec((L, wq), lambda b, c: (b * nc + c, qc)),
                  pl.BlockSpec((L, wq), lambda b, c: (b * nc + c, kc)),
                  pl.BlockSpec((L, wv), lambda b, c: (b * nc + c, vc)),
                  pl.BlockSpec((L, wv), lambda b, c: (b * nc + c, oc)),
                  pl.BlockSpec((L, LANES), lambda b, c: (b * nc + c, 0)),
                  pl.BlockSpec((None, SUBLANES, L), lambda b, c: (b, 0, c)),
                  pl.BlockSpec((CONV_W, wq), lambda b, c: (0, 0)),
                  pl.BlockSpec((CONV_W, wq), lambda b, c: (0, 1)),
                  pl.BlockSpec((1, wq), lambda b, c: (0, 0)),
                  pl.BlockSpec((1, wq), lambda b, c: (0, 1)),
                  pl.BlockSpec((1, wv), lambda b, c: (0, 0))],
        out_specs=pl.BlockSpec((L, wv), lambda b, c: (b * nc + c, 0)),
        out_shape=jax.ShapeDtypeStruct((m, dm.d_branch), BF16),
        scratch_shapes=[pltpu.VMEM((L + SUBLANES, wq), F32),
                        pltpu.VMEM((L + SUBLANES, wq), F32),
                        pltpu.VMEM((hm, ML_DQK, ML_DV), F32),
                        pltpu.VMEM((hm, 1, ML_DQK), F32),
                        pltpu.VMEM((hm, 1, 1), F32)],
        compiler_params=_cp("parallel", "arbitrary"),
        name="mlstm_branch",
    )(proj, proj, proj, proj, g, gt, cw, cw, cb, cb, norm_g.reshape(1, -1))


def _merge_kernel(ya_ref, yb_ref, wa_ref, wb_ref, sa_ref, sb_ref, o_ref, wabf_ref, wbbf_ref):
    @pl.when(pl.program_id(1) == 0)
    def _():
        wabf_ref[...] = wa_ref[...].astype(BF16)
        wbbf_ref[...] = wb_ref[...].astype(BF16)

    pa = jnp.dot(ya_ref[...], wabf_ref[...], preferred_element_type=F32)
    pb = jnp.dot(yb_ref[...], wbbf_ref[...], preferred_element_type=F32)
    o_ref[...] = (sa_ref[...].astype(F32) * pa + sb_ref[...].astype(F32) * pb).astype(o_ref.dtype)


def merge_branches(ya, yb, wa, wb, layer, sg, dm):
    m, kb = ya.shape
    d = dm.d_model
    tm, tn = dm.tm_dense, dm.tn
    nj = d // tn
    return pl.pallas_call(
        _merge_kernel,
        grid=(nj, m // tm),
        in_specs=[pl.BlockSpec((tm, kb), lambda j, i: (i, 0)),
                  pl.BlockSpec((tm, kb), lambda j, i: (i, 0)),
                  pl.BlockSpec((None, kb, tn), lambda j, i: (layer, 0, j)),
                  pl.BlockSpec((None, kb, tn), lambda j, i: (layer, 0, j)),
                  pl.BlockSpec((tm, tn), lambda j, i: (i, j)),
                  pl.BlockSpec((tm, tn), lambda j, i: (i, nj + j))],
        out_specs=pl.BlockSpec((tm, tn), lambda j, i: (i, j)),
        out_shape=jax.ShapeDtypeStruct((m, d), BF16),
        scratch_shapes=[pltpu.VMEM((kb, tn), BF16), pltpu.VMEM((kb, tn), BF16)],
        compiler_params=_cp("arbitrary", "arbitrary"),
        name="merge_branches",
    )(ya, yb, wa, wb, sg, sg)


def _mm_res_kernel(a_ref, w_ref, x_ref, gate_ref, o_ref, wbf_ref):
    @pl.when(pl.program_id(1) == 0)
    def _():
        wbf_ref[...] = w_ref[...].astype(BF16)

    acc = jnp.dot(a_ref[...], wbf_ref[...], preferred_element_type=F32)
    o_ref[...] = x_ref[...] + gate_ref[...] * acc


def mm_residual(a, w3, layer, x, mod3, gate_idx, dm):
    m, k = a.shape
    d = x.shape[1]
    wide = k <= dm.d_model
    tm = dm.tm_dense if wide else dm.tm
    tn = dm.tn_wide if wide else dm.tn
    nj = d // tn
    tpb = dm.seq // tm
    return pl.pallas_call(
        _mm_res_kernel,
        grid=(nj, m // tm),
        in_specs=[pl.BlockSpec((tm, k), lambda j, i: (i, 0)),
                  pl.BlockSpec((None, k, tn), lambda j, i: (layer, 0, j)),
                  pl.BlockSpec((tm, tn), lambda j, i: (i, j)),
                  pl.BlockSpec((None, 1, tn), lambda j, i: (i // tpb, 0, gate_idx * nj + j))],
        out_specs=pl.BlockSpec((tm, tn), lambda j, i: (i, j)),
        out_shape=jax.ShapeDtypeStruct((m, d), F32),
        scratch_shapes=[pltpu.VMEM((k, tn), BF16)],
        compiler_params=_cp("arbitrary", "arbitrary"),
        name="mm_residual",
    )(a, w3, x, mod3)


def _swiglu(a, w1, w3):
    p1 = jnp.dot(a, w1, preferred_element_type=F32)
    p3 = jnp.dot(a, w3, preferred_element_type=F32)
    return (p1 * _sigmoid(p1) * p3).astype(BF16)


def _up_kernel(a_ref, w1_ref, w3_ref, o_ref, w1bf_ref, w3bf_ref):
    @pl.when(pl.program_id(1) == 0)
    def _():
        w1bf_ref[...] = w1_ref[...].astype(BF16)
        w3bf_ref[...] = w3_ref[...].astype(BF16)

    o_ref[...] = _swiglu(a_ref[...], w1bf_ref[...], w3bf_ref[...])


def ffn_up(a, w1, w3, layer, dm):
    m, k = a.shape
    f = w1.shape[-1]
    tm, tn = dm.tm_dense, dm.tn
    return pl.pallas_call(
        _up_kernel,
        grid=(f // tn, m // tm),
        in_specs=[pl.BlockSpec((tm, k), lambda j, i: (i, 0)),
                  pl.BlockSpec((None, k, tn), lambda j, i: (layer, 0, j)),
                  pl.BlockSpec((None, k, tn), lambda j, i: (layer, 0, j))],
        out_specs=pl.BlockSpec((tm, tn), lambda j, i: (i, j)),
        out_shape=jax.ShapeDtypeStruct((m, f), BF16),
        scratch_shapes=[pltpu.VMEM((k, tn), BF16), pltpu.VMEM((k, tn), BF16)],
        compiler_params=_cp("arbitrary", "arbitrary"),
        name="ffn_up",
    )(a, w1, w3)


def _gup_kernel(te_ref, tv_ref, a_ref, w1_ref, w3_ref, o_ref, w1bf_ref, w3bf_ref):
    i = pl.program_id(1)
    changed = (i == 0) | (te_ref[i] != te_ref[jnp.maximum(i - 1, 0)])

    @pl.when(changed)
    def _():
        w1bf_ref[...] = w1_ref[...].astype(BF16)
        w3bf_ref[...] = w3_ref[...].astype(BF16)

    @pl.when(tv_ref[i] == 1)
    def _():
        o_ref[...] = _swiglu(a_ref[...], w1bf_ref[...], w3bf_ref[...])

    @pl.when(tv_ref[i] == 0)
    def _():
        o_ref[...] = jnp.zeros(o_ref.shape, o_ref.dtype)


def moe_up(hs, w1, w3, lm, te, tv, dm):
    p, k = hs.shape
    f = w1.shape[-1]
    tg, tn = dm.tg, dm.tn
    grid_spec = pltpu.PrefetchScalarGridSpec(
        num_scalar_prefetch=2,
        grid=(f // tn, p // tg),
        in_specs=[pl.BlockSpec((tg, k), lambda j, i, te, tv: (i, 0)),
                  pl.BlockSpec((None, None, k, tn), lambda j, i, te, tv: (lm, te[i], 0, j)),
                  pl.BlockSpec((None, None, k, tn), lambda j, i, te, tv: (lm, te[i], 0, j))],
        out_specs=pl.BlockSpec((tg, tn), lambda j, i, te, tv: (i, j)),
        scratch_shapes=[pltpu.VMEM((k, tn), BF16), pltpu.VMEM((k, tn), BF16)])
    return pl.pallas_call(
        _gup_kernel,
        grid_spec=grid_spec,
        out_shape=jax.ShapeDtypeStruct((p, f), BF16),
        compiler_params=_cp("arbitrary", "arbitrary"),
        name="moe_up",
    )(te, tv, hs, w1, w3)


def _gdown_kernel(te_ref, tv_ref, u_ref, w_ref, o_ref, wbf_ref):
    i = pl.program_id(1)
    changed = (i == 0) | (te_ref[i] != te_ref[jnp.maximum(i - 1, 0)])

    @pl.when(changed)
    def _():
        wbf_ref[...] = w_ref[...].astype(BF16)

    @pl.when(tv_ref[i] == 1)
    def _():
        o_ref[...] = jnp.dot(u_ref[...], wbf_ref[...], preferred_element_type=F32)

    @pl.when(tv_ref[i] == 0)
    def _():
        o_ref[...] = jnp.zeros(o_ref.shape, o_ref.dtype)


def moe_down(u, w2, lm, te, tv, dm):
    p, f = u.shape
    d = w2.shape[-1]
    tg, tn = dm.tg_down, dm.tn
    grid_spec = pltpu.PrefetchScalarGridSpec(
        num_scalar_prefetch=2,
        grid=(d // tn, p // tg),
        in_specs=[pl.BlockSpec((tg, f), lambda j, i, te, tv: (i, 0)),
                  pl.BlockSpec((None, None, f, tn), lambda j, i, te, tv: (lm, te[i], 0, j))],
        out_specs=pl.BlockSpec((tg, tn), lambda j, i, te, tv: (i, j)),
        scratch_shapes=[pltpu.VMEM((f, tn), BF16)])
    return pl.pallas_call(
        _gdown_kernel,
        grid_spec=grid_spec,
        out_shape=jax.ShapeDtypeStruct((p, d), F32),
        compiler_params=_cp("arbitrary", "arbitrary"),
        name="moe_down",
    )(te, tv, u, w2)


def _router_kernel(a_ref, w_ref, o_ref, *, n_experts):
    logits = jnp.dot(a_ref[...].astype(BF16), w_ref[...].astype(BF16),
                     preferred_element_type=F32)
    lane = lax.broadcasted_iota(jnp.int32, logits.shape, 1)
    lg = jnp.where(lane < n_experts, logits, NEG)
    v1 = jnp.max(lg, axis=1, keepdims=True)
    i1 = jnp.min(jnp.where(lg == v1, lane, LANES), axis=1, keepdims=True)
    lg2 = jnp.where(lane == i1, NEG, lg)
    v2 = jnp.max(lg2, axis=1, keepdims=True)
    i2 = jnp.min(jnp.where(lg2 == v2, lane, LANES), axis=1, keepdims=True)
    e2 = jnp.exp(v2 - v1)
    g1 = 1.0 / (1.0 + e2)
    g2 = e2 / (1.0 + e2)
    out = jnp.where(lane == 0, g1, 0.0)
    out = jnp.where(lane == 1, g2, out)
    out = jnp.where(lane == 2, i1.astype(F32), out)
    out = jnp.where(lane == 3, i2.astype(F32), out)
    o_ref[...] = out


def router(hf, router_w, lm, dm):
    m, k = hf.shape
    e = dm.n_experts
    tm = dm.tm
    wpad = jnp.zeros((k, LANES), F32).at[:, :e].set(router_w[lm])
    return pl.pallas_call(
        functools.partial(_router_kernel, n_experts=e),
        grid=(m // tm,),
        in_specs=[pl.BlockSpec((tm, k), lambda i: (i, 0)),
                  pl.BlockSpec((k, LANES), lambda i: (0, 0))],
        out_specs=pl.BlockSpec((tm, LANES), lambda i: (i, 0)),
        out_shape=jax.ShapeDtypeStruct((m, LANES), F32),
        compiler_params=_cp("parallel"),
        name="router",
    )(hf, wpad)


def _row_copy(src_ref, dst_ref, src_row, dst_row, sem):
    return pltpu.make_async_copy(src_ref.at[pl.ds(src_row, 1), :],
                                 dst_ref.at[pl.ds(dst_row, 1), :], sem)


def _gather_kernel(idx_ref, src_ref, o_ref, buf_ref, sem, *, rows):
    def issue(r, carry):
        _row_copy(src_ref, buf_ref, idx_ref[0, 0, r], r, sem).start()
        return carry

    lax.fori_loop(0, rows, issue, 0, unroll=DMA_UNROLL)

    def wait(r, carry):
        _row_copy(src_ref, buf_ref, 0, r, sem).wait()
        return carry

    lax.fori_loop(0, rows, wait, 0, unroll=DMA_UNROLL)
    o_ref[...] = buf_ref[...].astype(o_ref.dtype)


def gather_rows(src, idx, rows, out_dtype):
    p = idx.shape[0]
    d = src.shape[1]
    nblk = p // rows
    return pl.pallas_call(
        functools.partial(_gather_kernel, rows=rows),
        grid=(nblk,),
        in_specs=[pl.BlockSpec((1, 1, rows), lambda i: (i, 0, 0), memory_space=pltpu.SMEM),
                  pl.BlockSpec(memory_space=pl.ANY)],
        out_specs=pl.BlockSpec((rows, d), lambda i: (i, 0)),
        out_shape=jax.ShapeDtypeStruct((p, d), out_dtype),
        scratch_shapes=[pltpu.VMEM((rows, d), src.dtype), pltpu.SemaphoreType.DMA(())],
        compiler_params=_cp("arbitrary"),
        name="gather_rows",
    )(idx.reshape(nblk, 1, rows), src)


def _combine_kernel(pos_ref, ys_ref, x_ref, r_ref, gate_ref, o_ref, buf_ref, sem, *, rows):
    def issue(r, carry):
        _row_copy(ys_ref, buf_ref.at[0], pos_ref[0, 0, 2 * r], r, sem).start()
        _row_copy(ys_ref, buf_ref.at[1], pos_ref[0, 0, 2 * r + 1], r, sem).start()
        return carry

    lax.fori_loop(0, rows, issue, 0, unroll=DMA_UNROLL)

    def wait(r, carry):
        _row_copy(ys_ref, buf_ref.at[0], 0, r, sem).wait()
        _row_copy(ys_ref, buf_ref.at[1], 0, r, sem).wait()
        return carry

    lax.fori_loop(0, rows, wait, 0, unroll=DMA_UNROLL)
    rr = r_ref[...]
    mix = rr[:, 0:1] * buf_ref[0] + rr[:, 1:2] * buf_ref[1]
    o_ref[...] = x_ref[...] + gate_ref[...] * mix


def moe_combine(ys, pos, x, rinfo, mod3, gate_idx, dm):
    m, d = x.shape
    rows = dm.tc
    nblk = m // rows
    tpb = dm.seq // rows
    return pl.pallas_call(
        functools.partial(_combine_kernel, rows=rows),
        grid=(nblk,),
        in_specs=[pl.BlockSpec((1, 1, 2 * rows), lambda i: (i, 0, 0), memory_space=pltpu.SMEM),
                  pl.BlockSpec(memory_space=pl.ANY),
                  pl.BlockSpec((rows, d), lambda i: (i, 0)),
                  pl.BlockSpec((rows, LANES), lambda i: (i, 0)),
                  pl.BlockSpec((None, 1, d), lambda i: (i // tpb, 0, gate_idx))],
        out_specs=pl.BlockSpec((rows, d), lambda i: (i, 0)),
        out_shape=jax.ShapeDtypeStruct((m, d), F32),
        scratch_shapes=[pltpu.VMEM((2, rows, d), F32), pltpu.SemaphoreType.DMA(())],
        compiler_params=_cp("arbitrary"),
        name="moe_combine",
    )(pos.reshape(nblk, 1, 2 * rows), ys, x, rinfo, mod3)


def routing_tables(rinfo, dm):
    m = rinfo.shape[0]
    e, tg = dm.n_experts, dm.tg
    n_tiles = (TOP_K * m) // tg + e
    eid = rinfo[:, 2:2 + TOP_K].astype(jnp.int32).reshape(-1)
    onehot = (eid[:, None] == jnp.arange(e, dtype=jnp.int32)[None, :]).astype(jnp.int32)
    csum = jnp.cumsum(onehot, axis=0)
    rank = jnp.sum((csum - onehot) * onehot, axis=1)
    counts = csum[-1]
    tiles_per = (counts + tg - 1) // tg
    tile_end = jnp.cumsum(tiles_per)
    tile_start = tile_end - tiles_per
    pos = tile_start[eid] * tg + rank
    token = jnp.arange(TOP_K * m, dtype=jnp.int32) // TOP_K
    slot_token = jnp.zeros((n_tiles * tg,), jnp.int32).at[pos].set(token)
    group_end = tile_end * tg
    row_end = tile_start * tg + counts
    used_rows = group_end[-1]
    last_e = jnp.sum((jnp.maximum(used_rows - 1, 0) >= group_end).astype(jnp.int32))

    def tile_tables(t):
        row0 = jnp.arange(n_tiles * tg // t, dtype=jnp.int32) * t
        e_raw = jnp.sum((row0[:, None] >= group_end[None, :]).astype(jnp.int32), axis=1)
        in_use = row0 < used_rows
        te = jnp.where(in_use, e_raw, last_e).astype(jnp.int32)
        tv = (in_use & (row0 < row_end[te])).astype(jnp.int32)
        return te, tv

    return pos.astype(jnp.int32), slot_token, tile_tables(tg), tile_tables(dm.tg_down)


def forward(dm, x, c, rel_bias, w_mod, b_mod, norm_mix_g, norm_ff_g, w_in,
            da_q_g, da_k_g, da_lam_q1, da_lam_k1, da_lam_q2, da_lam_k2, da_sub_g,
            ml_conv_w, ml_conv_b, ml_i_bias, ml_f_bias, ml_norm_g,
            w_branch_a, w_branch_b, w_out,
            ffn_w1, ffn_w3, ffn_w2, router_w, moe_w1, moe_w3, moe_w2):
    b, t, d = x.shape
    m = b * t
    h = dm.da_heads
    xf = x.astype(F32).reshape(m, d)
    mod_all = adaln_mod(c, w_mod, b_mod)
    d0, e0 = bias_tiles(rel_bias)
    c_gate = dm.n_main + 2 * dm.ml_heads
    w_g = w_in[:, :, c_gate:]

    for l in range(dm.depth):
        mod3 = mod_all[l].reshape(b, 1, 6 * d)
        hb = norm_mod(xf, norm_mix_g[l], mod3, 1, 0, dm, BF16)
        proj = mm_act(hb, w_in, l, dm.n_main, None, BF16, dm, "proj_main")
        sg = mm_act(hb, w_g, l, 2 * d, "sigmoid", BF16, dm, "branch_gates")
        g, gt = if_gates(hb, w_in, l, ml_i_bias[l], ml_f_bias[l], dm)
        lam_init = 0.8 - 0.6 * math.exp(-0.3 * l)
        lamv = jnp.stack([da_lam_q1[l], da_lam_k1[l], da_lam_q2[l], da_lam_k2[l]]).astype(F32)
        ya = diff_attention(proj, d0, e0, lamv, da_sub_g[l].astype(F32), da_q_g[l], da_k_g[l],
                            lam_init, dm)
        yb = mlstm_branch(proj, g, gt, ml_conv_w[l], ml_conv_b[l], ml_norm_g[l].astype(F32), dm)
        merged = merge_branches(ya, yb, w_branch_a, w_branch_b, l, sg, dm)
        xf = mm_residual(merged, w_out, l, xf, mod3, 2, dm)
        if l % 2 == 0:
            hb = norm_mod(xf, norm_ff_g[l], mod3, 4, 3, dm, BF16)
            u = ffn_up(hb, ffn_w1, ffn_w3, l // 2, dm)
            xf = mm_residual(u, ffn_w2, l // 2, xf, mod3, 5, dm)
        else:
            lm = l // 2
            hf = norm_mod(xf, norm_ff_g[l], mod3, 4, 3, dm, F32)
            rinfo = router(hf, router_w, lm, dm)
            pos, slot_token, (te_up, tv_up), (te_dn, tv_dn) = routing_tables(rinfo, dm)
            hs = gather_rows(hf, slot_token, dm.tg, BF16)
            u = moe_up(hs, moe_w1, moe_w3, lm, te_up, tv_up, dm)
            ys = moe_down(u, moe_w2, lm, te_dn, tv_dn, dm)
            xf = moe_combine(ys, pos, xf, rinfo, mod3, 5, dm)
    return xf.reshape(b, t, d)


def kernel(x, c, rel_bias, w_mod, b_mod, norm_mix_g, norm_ff_g, w_in, da_q_g, da_k_g, da_lam_q1, da_lam_k1, da_lam_q2, da_lam_k2, da_sub_g, ml_conv_w, ml_conv_b, ml_i_bias, ml_f_bias, ml_norm_g, w_branch_a, w_branch_b, w_out, ffn_w1, ffn_w3, ffn_w2, router_w, moe_w1, moe_w3, moe_w2):
    b, t, d = x.shape
    dm = Dims(d_model=d, batch=b, seq=t, depth=w_mod.shape[0], d_ff=ffn_w1.shape[-1],
              n_experts=router_w.shape[-1])
    return forward(dm, x, c, rel_bias, w_mod, b_mod, norm_mix_g, norm_ff_g, w_in,
                   da_q_g, da_k_g, da_lam_q1, da_lam_k1, da_lam_q2, da_lam_k2, da_sub_g,
                   ml_conv_w, ml_conv_b, ml_i_bias, ml_f_bias, ml_norm_g,
                   w_branch_a, w_branch_b, w_out,
                   ffn_w1, ffn_w3, ffn_w2, router_w, moe_w1, moe_w3, moe_w2)
```

```python
import functools
import math
from typing import NamedTuple

import numpy as np
import jax
import jax.numpy as jnp
from jax import lax
from jax.experimental import pallas as pl
from jax.experimental.pallas import tpu as pltpu

F32 = jnp.float32
BF16 = jnp.bfloat16

DA_DQK = 64
DA_DV = 128
ML_DQK = 128
ML_DV = 256
CONV_W = 4
TOP_K = 2
REL_BUCKETS = 32
REL_MAX_DIST = 128
EPS = 1e-6
NEG = -1e30
LOG2E = math.log2(math.e)
LANES = 128
SUBLANES = 8
VMEM_LIMIT = 52 * 1024 * 1024
DMA_UNROLL = 8


class Dims(NamedTuple):
    d_model: int
    batch: int
    seq: int
    depth: int
    d_ff: int
    n_experts: int
    tm: int = 512
    tm_dense: int = 1024
    tn: int = 512
    tn_wide: int = 1024
    tq: int = 1024
    ml_chunk: int = 256
    tg: int = 512
    tc: int = 256
    attn_rc: int = 2048

    @property
    def d_branch(self):
        return self.d_model // 2

    @property
    def da_heads(self):
        return self.d_branch // DA_DV

    @property
    def ml_heads(self):
        return self.d_branch // ML_DV

    @property
    def n_main(self):
        return 3 * self.d_branch + 2 * self.ml_heads * ML_DQK + 2 * self.d_branch

    @property
    def tokens(self):
        return self.batch * self.seq


def _cp(*sem):
    return pltpu.CompilerParams(dimension_semantics=sem, vmem_limit_bytes=VMEM_LIMIT)


def _sigmoid(x):
    return 1.0 / (1.0 + jnp.exp(-x))


def _log_sigmoid(x):
    return jnp.minimum(x, 0.0) - jnp.log(1.0 + jnp.exp(-jnp.abs(x)))


def _mod_kernel(c_ref, w_ref, b_ref, o_ref):
    c = c_ref[...]
    ca = c * _sigmoid(c)
    o_ref[...] = jnp.dot(ca, w_ref[...], preferred_element_type=F32) + b_ref[...]


def adaln_mod(c, w_mod, b_mod):
    depth, d, n = w_mod.shape
    b = c.shape[0]
    bp = -(-b // SUBLANES) * SUBLANES
    cp = jnp.zeros((bp, d), F32).at[:b].set(c.astype(F32))
    tn = min(n, 1024)
    out = pl.pallas_call(
        _mod_kernel,
        grid=(depth, n // tn),
        in_specs=[pl.BlockSpec((bp, d), lambda l, j: (0, 0)),
                  pl.BlockSpec((None, d, tn), lambda l, j: (l, 0, j)),
                  pl.BlockSpec((None, 1, tn), lambda l, j: (l, 0, j))],
        out_specs=pl.BlockSpec((None, bp, tn), lambda l, j: (l, 0, j)),
        out_shape=jax.ShapeDtypeStruct((depth, bp, n), F32),
        compiler_params=_cp("parallel", "parallel"),
        name="adaln_mod",
    )(cp, w_mod, b_mod.reshape(depth, 1, n))
    return out[:, :b]


def _norm_mod_kernel(x_ref, g_ref, sc_ref, sh_ref, o_ref):
    x = x_ref[...]
    ms = jnp.mean(x * x, axis=-1, keepdims=True)
    h = x * lax.rsqrt(ms + EPS) * g_ref[...] * (1.0 + sc_ref[...]) + sh_ref[...]
    o_ref[...] = h.astype(o_ref.dtype)


def norm_mod(x, g, mod3, scale_idx, shift_idx, dm, out_dtype):
    m, d = x.shape
    tm = dm.tm_dense
    tpb = dm.seq // tm
    return pl.pallas_call(
        _norm_mod_kernel,
        grid=(m // tm,),
        in_specs=[pl.BlockSpec((tm, d), lambda i: (i, 0)),
                  pl.BlockSpec((1, d), lambda i: (0, 0)),
                  pl.BlockSpec((None, 1, d), lambda i: (i // tpb, 0, scale_idx)),
                  pl.BlockSpec((None, 1, d), lambda i: (i // tpb, 0, shift_idx))],
        out_specs=pl.BlockSpec((tm, d), lambda i: (i, 0)),
        out_shape=jax.ShapeDtypeStruct((m, d), out_dtype),
        compiler_params=_cp("parallel"),
        name="norm_mod",
    )(x, g.reshape(1, d), mod3, mod3)


def _mm_act_kernel(a_ref, w_ref, o_ref, wbf_ref, *, act):
    @pl.when(pl.program_id(1) == 0)
    def _():
        wbf_ref[...] = w_ref[...].astype(BF16)

    acc = jnp.dot(a_ref[...], wbf_ref[...], preferred_element_type=F32)
    if act == "sigmoid":
        acc = _sigmoid(acc)
    o_ref[...] = acc.astype(o_ref.dtype)


def mm_act(a, w3, layer, n, act, out_dtype, dm, name):
    m, k = a.shape
    tm, tn = dm.tm_dense, dm.tn_wide
    return pl.pallas_call(
        functools.partial(_mm_act_kernel, act=act),
        grid=(n // tn, m // tm),
        in_specs=[pl.BlockSpec((tm, k), lambda j, i: (i, 0)),
                  pl.BlockSpec((None, k, tn), lambda j, i: (layer, 0, j))],
        out_specs=pl.BlockSpec((tm, tn), lambda j, i: (i, j)),
        out_shape=jax.ShapeDtypeStruct((m, n), out_dtype),
        scratch_shapes=[pltpu.VMEM((k, tn), BF16)],
        compiler_params=_cp("arbitrary", "arbitrary"),
        name=name,
    )(a, w3)


def _ifgate_kernel(a_ref, w_ref, wt_ref, bc_ref, br_ref, g_ref, gt_ref, *, hm):
    a = a_ref[...]
    gc = jnp.dot(a, w_ref[...].astype(BF16), preferred_element_type=F32) + bc_ref[...]
    lane = lax.broadcasted_iota(jnp.int32, gc.shape, 1)
    g_ref[...] = jnp.where((lane >= hm) & (lane < 2 * hm), _log_sigmoid(gc), gc)
    gt = lax.dot_general(wt_ref[...].astype(BF16), a, (((1,), (1,)), ((), ())),
                         preferred_element_type=F32) + br_ref[...]
    row = lax.broadcasted_iota(jnp.int32, gt.shape, 0)
    gt_ref[...] = jnp.where((row >= hm) & (row < 2 * hm), _log_sigmoid(gt), gt)


def if_gates(h, w_in, layer, i_bias, f_bias, dm):
    m, k = h.shape
    hm = dm.ml_heads
    tm = dm.tm
    tpb = dm.seq // tm
    c0 = dm.n_main
    wt = jnp.zeros((SUBLANES, k), F32).at[:2 * hm].set(w_in[layer, :, c0:c0 + 2 * hm].T)
    bias = jnp.concatenate([i_bias, f_bias]).astype(F32)
    bc = jnp.zeros((1, LANES), F32).at[0, :2 * hm].set(bias)
    br = jnp.zeros((SUBLANES, 1), F32).at[:2 * hm, 0].set(bias)
    return pl.pallas_call(
        functools.partial(_ifgate_kernel, hm=hm),
        grid=(m // tm,),
        in_specs=[pl.BlockSpec((tm, k), lambda i: (i, 0)),
                  pl.BlockSpec((None, k, LANES), lambda i: (layer, 0, c0 // LANES)),
                  pl.BlockSpec((SUBLANES, k), lambda i: (0, 0)),
                  pl.BlockSpec((1, LANES), lambda i: (0, 0)),
                  pl.BlockSpec((SUBLANES, 1), lambda i: (0, 0))],
        out_specs=[pl.BlockSpec((tm, LANES), lambda i: (i, 0)),
                   pl.BlockSpec((None, SUBLANES, tm), lambda i: (i // tpb, 0, i % tpb))],
        out_shape=[jax.ShapeDtypeStruct((m, LANES), F32),
                   jax.ShapeDtypeStruct((dm.batch, SUBLANES, dm.seq), F32)],
        compiler_params=_cp("parallel"),
        name="if_gates",
    )(h, w_in, wt, bc, br)


def _t5_bucket_np(rel):
    n = jnp.maximum(rel, 0)
    max_exact = REL_BUCKETS // 2
    large = max_exact + (jnp.log(jnp.maximum(n, 1).astype(F32) / max_exact)
                         / math.log(REL_MAX_DIST / max_exact)
                         * (REL_BUCKETS - max_exact)).astype(jnp.int32)
    large = jnp.minimum(large, REL_BUCKETS - 1)
    return jnp.where(n < max_exact, n, large)


def _bias_tile_kernel(tab_ref, idd_ref, ide_ref, d_ref, e_ref):
    h2 = pl.program_id(0)
    far = tab_ref[REL_BUCKETS - 1, h2]
    idd = idd_ref[...]
    ide = ide_ref[...]
    d = jnp.zeros(idd.shape, F32)
    e = jnp.zeros(ide.shape, F32)
    for b in range(REL_BUCKETS):
        t = tab_ref[b, h2] - far
        d = jnp.where(idd == b, t, d)
        e = jnp.where(ide == b, t, e)
    r = lax.broadcasted_iota(jnp.int32, idd.shape, 0)
    c = lax.broadcasted_iota(jnp.int32, idd.shape, 1)
    d_ref[...] = jnp.where(r <= c, d * LOG2E, NEG)
    e_ref[...] = e * LOG2E


def bias_tiles(rel_bias):
    nb, h2 = rel_bias.shape
    r = jnp.arange(LANES, dtype=jnp.int32)[:, None]
    c = jnp.arange(LANES, dtype=jnp.int32)[None, :]
    idd = _t5_bucket_np(c - r)
    ide = _t5_bucket_np(LANES + c - r)
    spec = pl.BlockSpec((LANES, LANES), lambda h: (0, 0))
    ospec = pl.BlockSpec((None, LANES, LANES), lambda h: (h, 0, 0))
    return pl.pallas_call(
        _bias_tile_kernel,
        grid=(h2,),
        in_specs=[pl.BlockSpec(memory_space=pltpu.SMEM), spec, spec],
        out_specs=[ospec, ospec],
        out_shape=[jax.ShapeDtypeStruct((h2, LANES, LANES), F32)] * 2,
        compiler_params=_cp("arbitrary"),
        name="bias_tiles",
    )(rel_bias.astype(F32), idd, ide)


V_ROWS = DA_DV + 16
BOUND_MARGIN = 1.02
MAX_BOUND_SPREAD = 100.0


def _attn_kernel(q_ref, k_ref, v_ref, d0_ref, e0_ref, lam_ref, subg_ref, gq_ref, gk_ref, o_ref,
                 qq_ref, kn_ref, vt_ref, m_ref, acc_ref, bd_ref, be_ref, stat_ref,
                 *, tq, rc, lam_init):
    qi = pl.program_id(2)
    nsub = tq // LANES

    @pl.when(qi == 0)
    def _():
        for mp in range(2):
            for a in range(nsub):
                for b in range(nsub):
                    rows = pl.ds(b * LANES, LANES)
                    cols = pl.ds(mp * tq + a * LANES, LANES)
                    if a == b:
                        bd_ref[rows, cols] = d0_ref[mp]
                    elif a == b + 1:
                        bd_ref[rows, cols] = e0_ref[mp]
                    elif a < b:
                        bd_ref[rows, cols] = jnp.full((LANES, LANES), NEG, F32)
                    else:
                        bd_ref[rows, cols] = jnp.zeros((LANES, LANES), F32)
                    if a == 0 and b == nsub - 1:
                        be_ref[rows, cols] = e0_ref[mp]
                    else:
                        be_ref[rows, cols] = jnp.zeros((LANES, LANES), F32)

        def prep_kv(ci, carry):
            rows = pl.ds(pl.multiple_of(ci * tq, tq), tq)
            kf = k_ref[rows, :].astype(F32)
            low = lax.broadcasted_iota(jnp.int32, kf.shape, 1) < DA_DQK
            sq = kf * kf
            ss1 = jnp.sum(jnp.where(low, sq, 0.0), axis=1, keepdims=True)
            ss2 = jnp.sum(jnp.where(low, 0.0, sq), axis=1, keepdims=True)
            rs = lax.rsqrt(jnp.where(low, ss1, ss2) * (1.0 / DA_DQK) + EPS)
            kn = (kf * rs * gk_ref[...]).astype(BF16)
            kn_ref[rows, :] = kn
            kq = kn.astype(F32) * kn.astype(F32)
            n1 = jnp.sum(jnp.where(low, kq, 0.0), axis=1, keepdims=True)
            n2 = jnp.sum(jnp.where(low, 0.0, kq), axis=1, keepdims=True)
            vt_ref[ci, pl.ds(0, DA_DV), :] = v_ref[rows, :].astype(F32).T.astype(BF16)
            pad_row = lax.broadcasted_iota(jnp.int32, (V_ROWS - DA_DV, tq), 0)
            vt_ref[ci, pl.ds(DA_DV, V_ROWS - DA_DV), :] = jnp.where(pad_row == 0, 1.0, 0.0).astype(BF16)
            return jnp.maximum(carry, jnp.max(jnp.maximum(n1, n2), axis=0, keepdims=True))

        ksq_max = lax.fori_loop(0, k_ref.shape[0] // tq, prep_kv, jnp.zeros((1, 1), F32))
        d0v = d0_ref[...]
        e0v = e0_ref[...]
        valid = d0v > 0.5 * NEG
        b_hi = jnp.maximum(jnp.maximum(jnp.max(jnp.where(valid, d0v, 0.0)), jnp.max(e0v)), 0.0)
        b_lo = jnp.minimum(jnp.minimum(jnp.min(jnp.where(valid, d0v, 0.0)), jnp.min(e0v)), 0.0)
        stat_ref[0:1, :] = jnp.broadcast_to(jnp.sqrt(ksq_max), (1, LANES))
        stat_ref[1:2, :] = jnp.full((1, LANES), b_hi, F32)
        stat_ref[2:3, :] = jnp.full((1, LANES), b_hi - b_lo, F32)

    qf = q_ref[...].astype(F32)
    lane = lax.broadcasted_iota(jnp.int32, qf.shape, 1)
    for mp in range(2):
        qm = jnp.where((lane < DA_DQK) == (mp == 0), qf, 0.0)
        ss = jnp.sum(qm * qm, axis=1, keepdims=True)
        qn = qm * lax.rsqrt(ss * (1.0 / DA_DQK) + EPS) * gq_ref[...]
        qq_ref[pl.ds(mp * tq, tq), :] = qn.astype(BF16)
    acc_ref[...] = jnp.zeros(acc_ref.shape, F32)

    nchunk = 2 * tq // rc

    qq = qq_ref[...].astype(F32)
    qn2 = lax.dot_general(jnp.ones((SUBLANES, LANES), BF16), (qq * qq).astype(BF16),
                          (((1,), (1,)), ((), ())), preferred_element_type=F32)[0:1]
    m_bound = jnp.sqrt(qn2) * (BOUND_MARGIN * stat_ref[0:1, 0:1]) + stat_ref[1:2, 0:1]
    spread = 2.0 * jnp.max(m_bound, axis=1, keepdims=True) + stat_ref[2:3, 0:1]
    bounded = spread[0, 0] <= MAX_BOUND_SPREAD

    all_cols = [(c * rc, rc) for c in range(nchunk)]
    half = tq // 2

    def update(ki, bias_ref, online, k0=0, klen=tq, col_ranges=all_cols):
        k = kn_ref[pl.ds(pl.multiple_of(ki * tq + k0, half), klen), :]
        vt = vt_ref[ki][:, k0:k0 + klen]
        ss = [lax.dot_general(k, qq_ref[pl.ds(c0, cl), :], (((1,), (1,)), ((), ())),
                              preferred_element_type=F32) for c0, cl in col_ranges]
        m_prev_all = m_ref[...]
        results = []
        for s, (c0, cl) in zip(ss, col_ranges):
            if bias_ref is not None:
                s = s + bias_ref[pl.ds(k0, klen), pl.ds(c0, cl)]
            m_prev = m_prev_all[:, c0:c0 + cl]
            m_new = jnp.maximum(m_prev, jnp.max(s, axis=0, keepdims=True)) if online else m_prev
            p = jnp.exp2(s - m_new).astype(BF16)
            pv = jnp.dot(vt, p, preferred_element_type=F32)
            results.append((m_new, jnp.exp2(m_prev - m_new) if online else None, pv))
        for (m_new, alpha, pv), (c0, cl) in zip(results, col_ranges):
            cols = pl.ds(c0, cl)
            if online:
                acc_ref[:, cols] = alpha * acc_ref[:, cols] + pv
                m_ref[:, cols] = m_new
            else:
                acc_ref[:, cols] = acc_ref[:, cols] + pv

    def attend(online):
        def far_body(ki, carry):
            update(ki, None, online)
            return carry

        lax.fori_loop(0, jnp.maximum(qi - 1, 0), far_body, 0)

        @pl.when(qi >= 1)
        def _():
            update(qi - 1, be_ref, online)

        update(qi, bd_ref, online, 0, half)
        update(qi, bd_ref, online, half, half, [(half, half), (tq + half, half)])

    @pl.when(bounded)
    def _():
        m_ref[...] = m_bound
        attend(False)

    @pl.when(jnp.logical_not(bounded))
    def _():
        m_ref[...] = jnp.full(m_ref.shape, NEG, F32)
        attend(True)

    acc = acc_ref[...]
    o = acc[:DA_DV] / acc[DA_DV:DA_DV + 1]
    lamv = lam_ref[...]
    lam = (jnp.exp(jnp.sum(lamv[0:1] * lamv[1:2], axis=1, keepdims=True))
           - jnp.exp(jnp.sum(lamv[2:3] * lamv[3:4], axis=1, keepdims=True)) + lam_init)
    od = o[:, :tq] - lam * o[:, tq:]
    ms = jnp.mean(od * od, axis=0, keepdims=True)
    y = od * lax.rsqrt(ms + EPS) * subg_ref[...] * (1.0 - lam_init)
    o_ref[...] = y.T.astype(o_ref.dtype)


def diff_attention(proj, d0, e0, lamv, sub_g, q_g, k_g, lam_init, dm):
    m = proj.shape[0]
    t, tq, h = dm.seq, dm.tq, dm.da_heads
    nq = t // tq
    gq = (jnp.tile(q_g.astype(F32), 2) * (DA_DQK ** -0.5 * LOG2E)).reshape(1, LANES)
    gk = jnp.tile(k_g.astype(F32), 2).reshape(1, LANES)
    vec = pl.BlockSpec((1, LANES), lambda b, hh, qi: (0, 0))
    return pl.pallas_call(
        functools.partial(_attn_kernel, tq=tq, rc=min(dm.attn_rc, 2 * tq), lam_init=lam_init),
        grid=(dm.batch, h, nq),
        in_specs=[pl.BlockSpec((tq, LANES), lambda b, hh, qi: (b * nq + qi, hh)),
                  pl.BlockSpec((t, LANES), lambda b, hh, qi: (b, h + hh)),
                  pl.BlockSpec((t, LANES), lambda b, hh, qi: (b, 2 * h + hh)),
                  pl.BlockSpec((2, LANES, LANES), lambda b, hh, qi: (hh, 0, 0)),
                  pl.BlockSpec((2, LANES, LANES), lambda b, hh, qi: (hh, 0, 0)),
                  pl.BlockSpec((4, DA_DQK), lambda b, hh, qi: (0, 0)),
                  pl.BlockSpec((DA_DV, 1), lambda b, hh, qi: (0, 0)),
                  vec, vec],
        out_specs=pl.BlockSpec((tq, LANES), lambda b, hh, qi: (b * nq + qi, hh)),
        out_shape=jax.ShapeDtypeStruct((m, dm.d_branch), BF16),
        scratch_shapes=[pltpu.VMEM((2 * tq, LANES), BF16),
                        pltpu.VMEM((t, LANES), BF16),
                        pltpu.VMEM((nq, V_ROWS, tq), BF16),
                        pltpu.VMEM((1, 2 * tq), F32),
                        pltpu.VMEM((V_ROWS, 2 * tq), F32),
                        pltpu.VMEM((tq, 2 * tq), F32),
                        pltpu.VMEM((tq, 2 * tq), F32),
                        pltpu.VMEM((SUBLANES, LANES), F32)],
        compiler_params=_cp("parallel", "parallel", "arbitrary"),
        name="diff_attention",
    )(proj, proj, proj, d0, e0, lamv, sub_g.reshape(DA_DV, 1), gq, gk)


def _mlstm_kernel(q_ref, k_ref, v_ref, og_ref, g_ref, gt_ref, cwq_ref, cwk_ref,
                  cbq_ref, cbk_ref, ng_ref, o_ref,
                  qbuf_ref, kbuf_ref, ct_ref, n_ref, m_ref, *, L, hm):
    c = pl.program_id(1)

    @pl.when(c == 0)
    def _():
        qbuf_ref[pl.ds(0, SUBLANES), :] = jnp.zeros((SUBLANES, hm * ML_DQK), F32)
        kbuf_ref[pl.ds(0, SUBLANES), :] = jnp.zeros((SUBLANES, hm * ML_DQK), F32)
        ct_ref[...] = jnp.zeros(ct_ref.shape, F32)
        n_ref[...] = jnp.zeros(n_ref.shape, F32)
        m_ref[...] = jnp.zeros(m_ref.shape, F32)

    def conv_silu(x_ref, buf_ref, w_ref, b_ref):
        buf_ref[pl.ds(SUBLANES, L), :] = x_ref[...].astype(F32)
        w = w_ref[...]
        y = b_ref[...] + w[CONV_W - 1:CONV_W] * buf_ref[pl.ds(SUBLANES, L), :]
        for s in range(1, CONV_W):
            y = y + w[CONV_W - 1 - s:CONV_W - s] * buf_ref[pl.ds(SUBLANES - s, L), :]
        tail = buf_ref[pl.ds(L, SUBLANES), :]
        buf_ref[pl.ds(0, SUBLANES), :] = tail
        return y * _sigmoid(y)

    q_all = conv_silu(q_ref, qbuf_ref, cwq_ref, cbq_ref)
    k_all = conv_silu(k_ref, kbuf_ref, cwk_ref, cbk_ref) * (ML_DQK ** -0.5)
    v_all = v_ref[...]
    og_all = og_ref[...]
    ng_all = ng_ref[...]

    g = g_ref[...]
    gt = gt_ref[...]
    ii = lax.broadcasted_iota(jnp.int32, (L, L), 0)
    jj = lax.broadcasted_iota(jnp.int32, (L, L), 1)
    tril = (jj <= ii).astype(F32)
    triu = (ii <= jj).astype(F32)
    bcum = jnp.dot(tril, g, preferred_element_type=F32, precision=lax.Precision.HIGHEST)
    btcum = jnp.dot(gt, triu, preferred_element_type=F32, precision=lax.Precision.HIGHEST)
    lane = lax.broadcasted_iota(jnp.int32, g.shape, 1)
    row = lax.broadcasted_iota(jnp.int32, gt.shape, 0)
    ct_all = ct_ref[...]
    n_all = n_ref[...]
    m_all = m_ref[...]

    new_state = []
    for h in range(hm):
        q = q_all[:, h * ML_DQK:(h + 1) * ML_DQK]
        k = k_all[:, h * ML_DQK:(h + 1) * ML_DQK]
        v = v_all[:, h * ML_DV:(h + 1) * ML_DV]
        ig_col = jnp.sum(jnp.where(lane == h, g, 0.0), axis=1, keepdims=True)
        b_col = jnp.sum(jnp.where(lane == hm + h, bcum, 0.0), axis=1, keepdims=True)
        ig_row = jnp.sum(jnp.where(row == h, gt, 0.0), axis=0, keepdims=True)
        b_row = jnp.sum(jnp.where(row == hm + h, btcum, 0.0), axis=0, keepdims=True)

        m_old = m_all[h]
        dlog = jnp.where(jj <= ii, b_col - b_row + ig_row, NEG)
        m_inter = b_col + m_old
        m_s = jnp.maximum(m_inter, jnp.max(dlog, axis=1, keepdims=True))
        dmat = jnp.exp(dlog - m_s)
        qb = q.astype(BF16)
        kb = k.astype(BF16)
        qk = lax.dot_general(qb, kb, (((1,), (1,)), ((), ())), preferred_element_type=F32)
        w = dmat * qk
        inter = jnp.exp(m_inter - m_s)
        ct = ct_all[h]
        n_row = n_all[h]
        num = (jnp.dot(w.astype(BF16), v, preferred_element_type=F32)
               + inter * jnp.dot(qb, ct.astype(BF16), preferred_element_type=F32))
        den = (jnp.sum(w, axis=1, keepdims=True)
               + inter * jnp.sum(q * n_row, axis=1, keepdims=True))
        hh = num / jnp.maximum(jnp.abs(den), jnp.exp(-m_s))

        m_new = m_s[L - 1:L]
        b_last = b_col[L - 1:L]
        w_end = jnp.exp(b_last - b_col + ig_col - m_new)
        decay = jnp.exp(b_last + m_old - m_new)
        kw = k * w_end
        ct_new = decay * ct + lax.dot_general(kw.astype(BF16), v, (((0,), (0,)), ((), ())),
                                              preferred_element_type=F32)
        n_new = decay * n_row + jnp.sum(kw, axis=0, keepdims=True)

        ms = jnp.mean(hh * hh, axis=-1, keepdims=True)
        cols = slice(h * ML_DV, (h + 1) * ML_DV)
        y = hh * lax.rsqrt(ms + EPS) * ng_all[:, cols] * _sigmoid(og_all[:, cols].astype(F32))
        new_state.append((ct_new, n_new, m_new, y))

    for h, (ct_new, n_new, m_new, y) in enumerate(new_state):
        ct_ref[h] = ct_new
        n_ref[h] = n_new
        m_ref[h] = m_new
        o_ref[:, pl.ds(h * ML_DV, ML_DV)] = y.astype(o_ref.dtype)


def mlstm_branch(proj, g, gt, conv_w, conv_b, norm_g, dm):
    m = proj.shape[0]
    t, L, hm, h = dm.seq, dm.ml_chunk, dm.ml_heads, dm.da_heads
    nc = t // L
    wq, wv = hm * ML_DQK, hm * ML_DV
    qc, kc = 3 * dm.d_branch // wq, 3 * dm.d_branch // wq + 1
    vc, oc = (3 * dm.d_branch + 2 * wq) // wv, (3 * dm.d_branch + 2 * wq) // wv + 1
    cw = conv_w.astype(F32)
    cb = conv_b.astype(F32).reshape(1, -1)
    return pl.pallas_call(
        functools.partial(_mlstm_kernel, L=L, hm=hm),
        grid=(dm.batch, nc),
        in_specs=[pl.BlockSpec((L, wq), lambda b, c: (b * nc + c, qc)),
                  pl.BlockSpec((L, wq), lambda b, c: (b * nc + c, kc)),
                  pl.BlockSpec((L, wv), lambda b, c: (b * nc + c, vc)),
                  pl.BlockSpec((L, wv), lambda b, c: (b * nc + c, oc)),
                  pl.BlockSpec((L, LANES), lambda b, c: (b * nc + c, 0)),
                  pl.BlockSpec((None, SUBLANES, L), lambda b, c: (b, 0, c)),
                  pl.BlockSpec((CONV_W, wq), lambda b, c: (0, 0)),
                  pl.BlockSpec((CONV_W, wq), lambda b, c: (0, 1)),
                  pl.BlockSpec((1, wq), lambda b, c: (0, 0)),
                  pl.BlockSpec((1, wq), lambda b, c: (0, 1)),
                  pl.BlockSpec((1, wv), lambda b, c: (0, 0))],
        out_specs=pl.BlockSpec((L, wv), lambda b, c: (b * nc + c, 0)),
        out_shape=jax.ShapeDtypeStruct((m, dm.d_branch), BF16),
        scratch_shapes=[pltpu.VMEM((L + SUBLANES, wq), F32),
                        pltpu.VMEM((L + SUBLANES, wq), F32),
                        pltpu.VMEM((hm, ML_DQK, ML_DV), F32),
                        pltpu.VMEM((hm, 1, ML_DQK), F32),
                        pltpu.VMEM((hm, 1, 1), F32)],
        compiler_params=_cp("parallel", "arbitrary"),
        name="mlstm_branch",
    )(proj, proj, proj, proj, g, gt, cw, cw, cb, cb, norm_g.reshape(1, -1))


def _merge_kernel(ya_ref, yb_ref, wa_ref, wb_ref, sa_ref, sb_ref, o_ref, wabf_ref, wbbf_ref):
    @pl.when(pl.program_id(1) == 0)
    def _():
        wabf_ref[...] = wa_ref[...].astype(BF16)
        wbbf_ref[...] = wb_ref[...].astype(BF16)

    pa = jnp.dot(ya_ref[...], wabf_ref[...], preferred_element_type=F32)
    pb = jnp.dot(yb_ref[...], wbbf_ref[...], preferred_element_type=F32)
    o_ref[...] = (sa_ref[...].astype(F32) * pa + sb_ref[...].astype(F32) * pb).astype(o_ref.dtype)


def merge_branches(ya, yb, wa, wb, layer, sg, dm):
    m, kb = ya.shape
    d = dm.d_model
    tm, tn = dm.tm_dense, dm.tn
    nj = d // tn
    return pl.pallas_call(
        _merge_kernel,
        grid=(nj, m // tm),
        in_specs=[pl.BlockSpec((tm, kb), lambda j, i: (i, 0)),
                  pl.BlockSpec((tm, kb), lambda j, i: (i, 0)),
                  pl.BlockSpec((None, kb, tn), lambda j, i: (layer, 0, j)),
                  pl.BlockSpec((None, kb, tn), lambda j, i: (layer, 0, j)),
                  pl.BlockSpec((tm, tn), lambda j, i: (i, j)),
                  pl.BlockSpec((tm, tn), lambda j, i: (i, nj + j))],
        out_specs=pl.BlockSpec((tm, tn), lambda j, i: (i, j)),
        out_shape=jax.ShapeDtypeStruct((m, d), BF16),
        scratch_shapes=[pltpu.VMEM((kb, tn), BF16), pltpu.VMEM((kb, tn), BF16)],
        compiler_params=_cp("arbitrary", "arbitrary"),
        name="merge_branches",
    )(ya, yb, wa, wb, sg, sg)


def _mm_res_kernel(a_ref, w_ref, x_ref, gate_ref, o_ref, wbf_ref):
    @pl.when(pl.program_id(1) == 0)
    def _():
        wbf_ref[...] = w_ref[...].astype(BF16)

    acc = jnp.dot(a_ref[...], wbf_ref[...], preferred_element_type=F32)
    o_ref[...] = x_ref[...] + gate_ref[...] * acc


def mm_residual(a, w3, layer, x, mod3, gate_idx, dm):
    m, k = a.shape
    d = x.shape[1]
    wide = k <= dm.d_model
    tm = dm.tm_dense if wide else dm.tm
    tn = dm.tn_wide if wide else dm.tn
    nj = d // tn
    tpb = dm.seq // tm
    return pl.pallas_call(
        _mm_res_kernel,
        grid=(nj, m // tm),
        in_specs=[pl.BlockSpec((tm, k), lambda j, i: (i, 0)),
                  pl.BlockSpec((None, k, tn), lambda j, i: (layer, 0, j)),
                  pl.BlockSpec((tm, tn), lambda j, i: (i, j)),
                  pl.BlockSpec((None, 1, tn), lambda j, i: (i // tpb, 0, gate_idx * nj + j))],
        out_specs=pl.BlockSpec((tm, tn), lambda j, i: (i, j)),
        out_shape=jax.ShapeDtypeStruct((m, d), F32),
        scratch_shapes=[pltpu.VMEM((k, tn), BF16)],
        compiler_params=_cp("arbitrary", "arbitrary"),
        name="mm_residual",
    )(a, w3, x, mod3)


def _swiglu(a, w1, w3):
    p1 = jnp.dot(a, w1, preferred_element_type=F32)
    p3 = jnp.dot(a, w3, preferred_element_type=F32)
    return (p1 * _sigmoid(p1) * p3).astype(BF16)


def _up_kernel(a_ref, w1_ref, w3_ref, o_ref, w1bf_ref, w3bf_ref):
    @pl.when(pl.program_id(1) == 0)
    def _():
        w1bf_ref[...] = w1_ref[...].astype(BF16)
        w3bf_ref[...] = w3_ref[...].astype(BF16)

    o_ref[...] = _swiglu(a_ref[...], w1bf_ref[...], w3bf_ref[...])


def ffn_up(a, w1, w3, layer, dm):
    m, k = a.shape
    f = w1.shape[-1]
    tm, tn = dm.tm_dense, dm.tn
    return pl.pallas_call(
        _up_kernel,
        grid=(f // tn, m // tm),
        in_specs=[pl.BlockSpec((tm, k), lambda j, i: (i, 0)),
                  pl.BlockSpec((None, k, tn), lambda j, i: (layer, 0, j)),
                  pl.BlockSpec((None, k, tn), lambda j, i: (layer, 0, j))],
        out_specs=pl.BlockSpec((tm, tn), lambda j, i: (i, j)),
        out_shape=jax.ShapeDtypeStruct((m, f), BF16),
        scratch_shapes=[pltpu.VMEM((k, tn), BF16), pltpu.VMEM((k, tn), BF16)],
        compiler_params=_cp("arbitrary", "arbitrary"),
        name="ffn_up",
    )(a, w1, w3)


def _gup_kernel(te_ref, tv_ref, a_ref, w1_ref, w3_ref, o_ref, w1bf_ref, w3bf_ref):
    i = pl.program_id(1)
    changed = (i == 0) | (te_ref[i] != te_ref[jnp.maximum(i - 1, 0)])

    @pl.when(changed)
    def _():
        w1bf_ref[...] = w1_ref[...].astype(BF16)
        w3bf_ref[...] = w3_ref[...].astype(BF16)

    @pl.when(tv_ref[i] == 1)
    def _():
        o_ref[...] = _swiglu(a_ref[...], w1bf_ref[...], w3bf_ref[...])

    @pl.when(tv_ref[i] == 0)
    def _():
        o_ref[...] = jnp.zeros(o_ref.shape, o_ref.dtype)


def moe_up(hs, w1, w3, lm, te, tv, dm):
    p, k = hs.shape
    f = w1.shape[-1]
    tg, tn = dm.tg, dm.tn
    grid_spec = pltpu.PrefetchScalarGridSpec(
        num_scalar_prefetch=2,
        grid=(f // tn, p // tg),
        in_specs=[pl.BlockSpec((tg, k), lambda j, i, te, tv: (i, 0)),
                  pl.BlockSpec((None, None, k, tn), lambda j, i, te, tv: (lm, te[i], 0, j)),
                  pl.BlockSpec((None, None, k, tn), lambda j, i, te, tv: (lm, te[i], 0, j))],
        out_specs=pl.BlockSpec((tg, tn), lambda j, i, te, tv: (i, j)),
        scratch_shapes=[pltpu.VMEM((k, tn), BF16), pltpu.VMEM((k, tn), BF16)])
    return pl.pallas_call(
        _gup_kernel,
        grid_spec=grid_spec,
        out_shape=jax.ShapeDtypeStruct((p, f), BF16),
        compiler_params=_cp("arbitrary", "arbitrary"),
        name="moe_up",
    )(te, tv, hs, w1, w3)


def _gdown_kernel(te_ref, tv_ref, u_ref, w_ref, o_ref, wbf_ref):
    i = pl.program_id(1)
    changed = (i == 0) | (te_ref[i] != te_ref[jnp.maximum(i - 1, 0)])

    @pl.when(changed)
    def _():
        wbf_ref[...] = w_ref[...].astype(BF16)

    @pl.when(tv_ref[i] == 1)
    def _():
        o_ref[...] = jnp.dot(u_ref[...], wbf_ref[...], preferred_element_type=F32)

    @pl.when(tv_ref[i] == 0)
    def _():
        o_ref[...] = jnp.zeros(o_ref.shape, o_ref.dtype)


def moe_down(u, w2, lm, te, tv, dm):
    p, f = u.shape
    d = w2.shape[-1]
    tg, tn = dm.tg, dm.tn
    grid_spec = pltpu.PrefetchScalarGridSpec(
        num_scalar_prefetch=2,
        grid=(d // tn, p // tg),
        in_specs=[pl.BlockSpec((tg, f), lambda j, i, te, tv: (i, 0)),
                  pl.BlockSpec((None, None, f, tn), lambda j, i, te, tv: (lm, te[i], 0, j))],
        out_specs=pl.BlockSpec((tg, tn), lambda j, i, te, tv: (i, j)),
        scratch_shapes=[pltpu.VMEM((f, tn), BF16)])
    return pl.pallas_call(
        _gdown_kernel,
        grid_spec=grid_spec,
        out_shape=jax.ShapeDtypeStruct((p, d), F32),
        compiler_params=_cp("arbitrary", "arbitrary"),
        name="moe_down",
    )(te, tv, u, w2)


def _router_kernel(a_ref, w_ref, o_ref, *, n_experts):
    logits = jnp.dot(a_ref[...].astype(BF16), w_ref[...].astype(BF16),
                     preferred_element_type=F32)
    lane = lax.broadcasted_iota(jnp.int32, logits.shape, 1)
    lg = jnp.where(lane < n_experts, logits, NEG)
    v1 = jnp.max(lg, axis=1, keepdims=True)
    i1 = jnp.min(jnp.where(lg == v1, lane, LANES), axis=1, keepdims=True)
    lg2 = jnp.where(lane == i1, NEG, lg)
    v2 = jnp.max(lg2, axis=1, keepdims=True)
    i2 = jnp.min(jnp.where(lg2 == v2, lane, LANES), axis=1, keepdims=True)
    e2 = jnp.exp(v2 - v1)
    g1 = 1.0 / (1.0 + e2)
    g2 = e2 / (1.0 + e2)
    out = jnp.where(lane == 0, g1, 0.0)
    out = jnp.where(lane == 1, g2, out)
    out = jnp.where(lane == 2, i1.astype(F32), out)
    out = jnp.where(lane == 3, i2.astype(F32), out)
    o_ref[...] = out


def router(hf, router_w, lm, dm):
    m, k = hf.shape
    e = dm.n_experts
    tm = dm.tm
    wpad = jnp.zeros((k, LANES), F32).at[:, :e].set(router_w[lm])
    return pl.pallas_call(
        functools.partial(_router_kernel, n_experts=e),
        grid=(m // tm,),
        in_specs=[pl.BlockSpec((tm, k), lambda i: (i, 0)),
                  pl.BlockSpec((k, LANES), lambda i: (0, 0))],
        out_specs=pl.BlockSpec((tm, LANES), lambda i: (i, 0)),
        out_shape=jax.ShapeDtypeStruct((m, LANES), F32),
        compiler_params=_cp("parallel"),
        name="router",
    )(hf, wpad)


def _row_copy(src_ref, dst_ref, src_row, dst_row, sem):
    return pltpu.make_async_copy(src_ref.at[pl.ds(src_row, 1), :],
                                 dst_ref.at[pl.ds(dst_row, 1), :], sem)


def _gather_kernel(idx_ref, src_ref, o_ref, buf_ref, sem, *, rows):
    def issue(r, carry):
        _row_copy(src_ref, buf_ref, idx_ref[0, 0, r], r, sem).start()
        return carry

    lax.fori_loop(0, rows, issue, 0, unroll=DMA_UNROLL)

    def wait(r, carry):
        _row_copy(src_ref, buf_ref, 0, r, sem).wait()
        return carry

    lax.fori_loop(0, rows, wait, 0, unroll=DMA_UNROLL)
    o_ref[...] = buf_ref[...].astype(o_ref.dtype)


def gather_rows(src, idx, rows, out_dtype):
    p = idx.shape[0]
    d = src.shape[1]
    nblk = p // rows
    return pl.pallas_call(
        functools.partial(_gather_kernel, rows=rows),
        grid=(nblk,),
        in_specs=[pl.BlockSpec((1, 1, rows), lambda i: (i, 0, 0), memory_space=pltpu.SMEM),
                  pl.BlockSpec(memory_space=pl.ANY)],
        out_specs=pl.BlockSpec((rows, d), lambda i: (i, 0)),
        out_shape=jax.ShapeDtypeStruct((p, d), out_dtype),
        scratch_shapes=[pltpu.VMEM((rows, d), src.dtype), pltpu.SemaphoreType.DMA(())],
        compiler_params=_cp("arbitrary"),
        name="gather_rows",
    )(idx.reshape(nblk, 1, rows), src)


def _combine_kernel(pos_ref, ys_ref, x_ref, r_ref, gate_ref, o_ref, buf_ref, sem, *, rows):
    def issue(r, carry):
        _row_copy(ys_ref, buf_ref.at[0], pos_ref[0, 0, 2 * r], r, sem).start()
        _row_copy(ys_ref, buf_ref.at[1], pos_ref[0, 0, 2 * r + 1], r, sem).start()
        return carry

    lax.fori_loop(0, rows, issue, 0, unroll=DMA_UNROLL)

    def wait(r, carry):
        _row_copy(ys_ref, buf_ref.at[0], 0, r, sem).wait()
        _row_copy(ys_ref, buf_ref.at[1], 0, r, sem).wait()
        return carry

    lax.fori_loop(0, rows, wait, 0, unroll=DMA_UNROLL)
    rr = r_ref[...]
    mix = rr[:, 0:1] * buf_ref[0] + rr[:, 1:2] * buf_ref[1]
    o_ref[...] = x_ref[...] + gate_ref[...] * mix


def moe_combine(ys, pos, x, rinfo, mod3, gate_idx, dm):
    m, d = x.shape
    rows = dm.tc
    nblk = m // rows
    tpb = dm.seq // rows
    return pl.pallas_call(
        functools.partial(_combine_kernel, rows=rows),
        grid=(nblk,),
        in_specs=[pl.BlockSpec((1, 1, 2 * rows), lambda i: (i, 0, 0), memory_space=pltpu.SMEM),
                  pl.BlockSpec(memory_space=pl.ANY),
                  pl.BlockSpec((rows, d), lambda i: (i, 0)),
                  pl.BlockSpec((rows, LANES), lambda i: (i, 0)),
                  pl.BlockSpec((None, 1, d), lambda i: (i // tpb, 0, gate_idx))],
        out_specs=pl.BlockSpec((rows, d), lambda i: (i, 0)),
        out_shape=jax.ShapeDtypeStruct((m, d), F32),
        scratch_shapes=[pltpu.VMEM((2, rows, d), F32), pltpu.SemaphoreType.DMA(())],
        compiler_params=_cp("arbitrary"),
        name="moe_combine",
    )(pos.reshape(nblk, 1, 2 * rows), ys, x, rinfo, mod3)


def routing_tables(rinfo, dm):
    m = rinfo.shape[0]
    e, tg = dm.n_experts, dm.tg
    n_tiles = (TOP_K * m) // tg + e
    eid = rinfo[:, 2:2 + TOP_K].astype(jnp.int32).reshape(-1)
    onehot = (eid[:, None] == jnp.arange(e, dtype=jnp.int32)[None, :]).astype(jnp.int32)
    csum = jnp.cumsum(onehot, axis=0)
    rank = jnp.sum((csum - onehot) * onehot, axis=1)
    counts = csum[-1]
    tiles_per = (counts + tg - 1) // tg
    tile_end = jnp.cumsum(tiles_per)
    tile_start = tile_end - tiles_per
    pos = tile_start[eid] * tg + rank
    token = jnp.arange(TOP_K * m, dtype=jnp.int32) // TOP_K
    slot_token = jnp.zeros((n_tiles * tg,), jnp.int32).at[pos].set(token)
    tidx = jnp.arange(n_tiles, dtype=jnp.int32)
    used = tile_end[-1]
    te_raw = jnp.sum((tidx[:, None] >= tile_end[None, :]).astype(jnp.int32), axis=1)
    last_e = jnp.sum((jnp.maximum(used - 1, 0) >= tile_end).astype(jnp.int32))
    te = jnp.where(tidx < used, te_raw, last_e).astype(jnp.int32)
    tv = (tidx < used).astype(jnp.int32)
    return pos.astype(jnp.int32), slot_token, te, tv


def forward(dm, x, c, rel_bias, w_mod, b_mod, norm_mix_g, norm_ff_g, w_in,
            da_q_g, da_k_g, da_lam_q1, da_lam_k1, da_lam_q2, da_lam_k2, da_sub_g,
            ml_conv_w, ml_conv_b, ml_i_bias, ml_f_bias, ml_norm_g,
            w_branch_a, w_branch_b, w_out,
            ffn_w1, ffn_w3, ffn_w2, router_w, moe_w1, moe_w3, moe_w2):
    b, t, d = x.shape
    m = b * t
    h = dm.da_heads
    xf = x.astype(F32).reshape(m, d)
    mod_all = adaln_mod(c, w_mod, b_mod)
    d0, e0 = bias_tiles(rel_bias)
    c_gate = dm.n_main + 2 * dm.ml_heads
    w_g = w_in[:, :, c_gate:]

    for l in range(dm.depth):
        mod3 = mod_all[l].reshape(b, 1, 6 * d)
        hb = norm_mod(xf, norm_mix_g[l], mod3, 1, 0, dm, BF16)
        proj = mm_act(hb, w_in, l, dm.n_main, None, BF16, dm, "proj_main")
        sg = mm_act(hb, w_g, l, 2 * d, "sigmoid", BF16, dm, "branch_gates")
        g, gt = if_gates(hb, w_in, l, ml_i_bias[l], ml_f_bias[l], dm)
        lam_init = 0.8 - 0.6 * math.exp(-0.3 * l)
        lamv = jnp.stack([da_lam_q1[l], da_lam_k1[l], da_lam_q2[l], da_lam_k2[l]]).astype(F32)
        ya = diff_attention(proj, d0, e0, lamv, da_sub_g[l].astype(F32), da_q_g[l], da_k_g[l],
                            lam_init, dm)
        yb = mlstm_branch(proj, g, gt, ml_conv_w[l], ml_conv_b[l], ml_norm_g[l].astype(F32), dm)
        merged = merge_branches(ya, yb, w_branch_a, w_branch_b, l, sg, dm)
        xf = mm_residual(merged, w_out, l, xf, mod3, 2, dm)
        if l % 2 == 0:
            hb = norm_mod(xf, norm_ff_g[l], mod3, 4, 3, dm, BF16)
            u = ffn_up(hb, ffn_w1, ffn_w3, l // 2, dm)
            xf = mm_residual(u, ffn_w2, l // 2, xf, mod3, 5, dm)
        else:
            lm = l // 2
            hf = norm_mod(xf, norm_ff_g[l], mod3, 4, 3, dm, F32)
            rinfo = router(hf, router_w, lm, dm)
            pos, slot_token, te, tv = routing_tables(rinfo, dm)
            hs = gather_rows(hf, slot_token, dm.tg, BF16)
            u = moe_up(hs, moe_w1, moe_w3, lm, te, tv, dm)
            ys = moe_down(u, moe_w2, lm, te, tv, dm)
            xf = moe_combine(ys, pos, xf, rinfo, mod3, 5, dm)
    return xf.reshape(b, t, d)


def kernel(x, c, rel_bias, w_mod, b_mod, norm_mix_g, norm_ff_g, w_in, da_q_g, da_k_g, da_lam_q1, da_lam_k1, da_lam_q2, da_lam_k2, da_sub_g, ml_conv_w, ml_conv_b, ml_i_bias, ml_f_bias, ml_norm_g, w_branch_a, w_branch_b, w_out, ffn_w1, ffn_w3, ffn_w2, router_w, moe_w1, moe_w3, moe_w2):
    b, t, d = x.shape
    dm = Dims(d_model=d, batch=b, seq=t, depth=w_mod.shape[0], d_ff=ffn_w1.shape[-1],
              n_experts=router_w.shape[-1])
    return forward(dm, x, c, rel_bias, w_mod, b_mod, norm_mix_g, norm_ff_g, w_in,
                   da_q_g, da_k_g, da_lam_q1, da_lam_k1, da_lam_q2, da_lam_k2, da_sub_g,
                   ml_conv_w, ml_conv_b, ml_i_bias, ml_f_bias, ml_norm_g,
                   w_branch_a, w_branch_b, w_out,
                   ffn_w1, ffn_w3, ffn_w2, router_w, moe_w1, moe_w3, moe_w2)
```
